```python
import math
import jax, jax.numpy as jnp
from jax import lax
import numpy as np

D_MODEL = 1024
BATCH = 2
SEQ = 8192
DEPTH = 4

CHUNK = 64
N_A_LAYERS = DEPTH // 2
N_B_LAYERS = DEPTH - N_A_LAYERS

RET_HEADS = 8
RET_QK_DIM = D_MODEL // RET_HEADS
RET_V_DIM = 2 * RET_QK_DIM
RET_QK_WIDTH = RET_HEADS * RET_QK_DIM
RET_WIDTH = RET_HEADS * RET_V_DIM
RET_IN_WIDTH = 2 * RET_QK_WIDTH + 2 * RET_WIDTH

DIFF_HEADS = 8
DIFF_HEAD_DIM = D_MODEL // (2 * DIFF_HEADS)
DIFF_V_DIM = 2 * DIFF_HEAD_DIM
DIFF_QK_WIDTH = DIFF_HEADS * 2 * DIFF_HEAD_DIM
DIFF_WIDTH = DIFF_HEADS * DIFF_V_DIM
DIFF_IN_WIDTH = DIFF_QK_WIDTH + DIFF_WIDTH
KV_WIDTH = DIFF_QK_WIDTH + DIFF_WIDTH

ROPE_THETA = 10000.0
Q_BLOCK = 128
EPS = 1e-6

kernel_name = "yoco_retention_diffattn_sandwich"


def rmsnorm(x, g):
    xf = x.astype(jnp.float32)
    y = xf * lax.rsqrt(jnp.mean(xf * xf, axis=-1, keepdims=True) + EPS)
    return (y * g.astype(jnp.float32)).astype(x.dtype)


def rope(x):
    s, dim = x.shape[1], x.shape[-1]
    half = dim // 2
    inv = ROPE_THETA ** (-jnp.arange(half, dtype=jnp.float32) / half)
    ang = jnp.arange(s, dtype=jnp.float32)[:, None] * inv[None, :]
    shape = (1, s) + (1,) * (x.ndim - 3) + (half,)
    cos = jnp.cos(ang).reshape(shape)
    sin = jnp.sin(ang).reshape(shape)
    xf = x.astype(jnp.float32)
    x1, x2 = xf[..., :half], xf[..., half:]
    return jnp.concatenate([x1 * cos - x2 * sin, x1 * sin + x2 * cos], axis=-1).astype(x.dtype)


def retention(q, k, v):
    b, s, h, dk = q.shape
    dv = v.shape[-1]
    n = s // CHUNK
    log_gamma = jnp.log1p(-jnp.exp2(-5.0 - jnp.arange(h, dtype=jnp.float32)))
    pos = jnp.arange(CHUNK, dtype=jnp.float32)
    diff = pos[:, None] - pos[None, :]
    decay_mask = jnp.where(diff[None] >= 0,
                           jnp.exp(jnp.maximum(diff, 0.0)[None] * log_gamma[:, None, None]),
                           0.0)
    q_decay = jnp.exp((pos[None, :] + 1.0) * log_gamma[:, None])[:, :, None]
    k_decay = jnp.exp((CHUNK - 1.0 - pos[None, :]) * log_gamma[:, None])[:, :, None]
    chunk_decay = jnp.exp(CHUNK * log_gamma)[None, :, None, None]

    def to_chunks(t):
        return t.astype(jnp.float32).reshape(b, n, CHUNK, h, t.shape[-1]).transpose(1, 0, 3, 2, 4)

    qc = to_chunks(q)
    kc = to_chunks(k) * (dk ** -0.5)
    vc = to_chunks(v)
    scores = jnp.einsum('nbhid,nbhjd->nbhij', qc, kc) * decay_mask[None, None]
    intra = jnp.einsum('nbhij,nbhje->nbhie', scores, vc)

    def step(state, inp):
        q_i, k_i, v_i = inp
        inter = jnp.einsum('bhcd,bhde->bhce', q_i * q_decay, state)
        state = state * chunk_decay + jnp.einsum('bhcd,bhce->bhde', k_i * k_decay, v_i)
        return state, inter

    state0 = jnp.zeros((b, h, dk, dv), jnp.float32)
    _, inter = lax.scan(step, state0, (qc, kc, vc))
    o = (intra + inter).transpose(1, 0, 3, 2, 4).reshape(b, s, h, dv)
    mu = jnp.mean(o, axis=-1, keepdims=True)
    var = jnp.mean(jnp.square(o - mu), axis=-1, keepdims=True)
    return (o - mu) * lax.rsqrt(var + EPS)


def diff_attention(q, k, v, lam):
    b, s, h, _, dh = q.shape
    dv = v.shape[-1]
    nblk = s // Q_BLOCK
    scale = dh ** -0.5
    qb = q.reshape(b, nblk, Q_BLOCK, h, 2, dh).transpose(1, 0, 2, 3, 4, 5)
    kf = k.astype(jnp.float32)
    vf = v.astype(jnp.float32)
    key_chunk = jnp.arange(s) // CHUNK
    neg = jnp.finfo(jnp.float32).min

    def block(args):
        q_i, i = args
        q_chunk = (i * Q_BLOCK + jnp.arange(Q_BLOCK)) // CHUNK
        mask = key_chunk[None, :] <= q_chunk[:, None]
        sc = jnp.einsum('bqhtd,bkhtd->bhtqk', q_i.astype(jnp.float32), kf) * scale
        p = jax.nn.softmax(jnp.where(mask, sc, neg), axis=-1)
        w = p[:, :, 0] - lam * p[:, :, 1]
        return jnp.einsum('bhqk,bkhe->bqhe', w, vf)

    out = lax.map(block, (qb, jnp.arange(nblk)))
    return out.transpose(1, 0, 2, 3, 4).reshape(b, s, h, dv)


def setup_inputs(seed: int = 0) -> dict:
    key = jax.random.key(seed)
    ks = jax.random.split(key, 16)
    f32 = jnp.float32

    def nrm(k, shape, scale):
        return jax.random.normal(k, shape, f32) * scale

    return {
        "x": nrm(ks[0], (BATCH, SEQ, D_MODEL), 1.0),
        "pre_norm": 1.0 + nrm(ks[1], (DEPTH, D_MODEL), 0.02),
        "post_norm": 1.0 + nrm(ks[2], (DEPTH, D_MODEL), 0.02),
        "w_in_a": nrm(ks[3], (N_A_LAYERS, D_MODEL, RET_IN_WIDTH), D_MODEL ** -0.5),
        "w_out_a": nrm(ks[4], (N_A_LAYERS, RET_WIDTH, D_MODEL), RET_WIDTH ** -0.5),
        "kv_norm": 1.0 + nrm(ks[5], (D_MODEL,), 0.02),
        "w_kv": nrm(ks[6], (D_MODEL, KV_WIDTH), D_MODEL ** -0.5),
        "w_in_b": nrm(ks[7], (N_B_LAYERS, D_MODEL, DIFF_IN_WIDTH), D_MODEL ** -0.5),
        "lam_q1": nrm(ks[8], (N_B_LAYERS, DIFF_HEAD_DIM), 0.1),
        "lam_k1": nrm(ks[9], (N_B_LAYERS, DIFF_HEAD_DIM), 0.1),
        "lam_q2": nrm(ks[10], (N_B_LAYERS, DIFF_HEAD_DIM), 0.1),
        "lam_k2": nrm(ks[11], (N_B_LAYERS, DIFF_HEAD_DIM), 0.1),
        "sub_norm_b": 1.0 + nrm(ks[12], (N_B_LAYERS, DIFF_V_DIM), 0.02),
        "w_out_b": nrm(ks[13], (N_B_LAYERS, DIFF_WIDTH, D_MODEL), DIFF_WIDTH ** -0.5),
    }


def reference(x, pre_norm, post_norm, w_in_a, w_out_a, kv_norm, w_kv, w_in_b,
              lam_q1, lam_k1, lam_q2, lam_k2, sub_norm_b, w_out_b):
    b, s, _ = x.shape
    k_shared = None
    v_shared = None
    for layer in range(DEPTH):
        h = rmsnorm(x, pre_norm[layer])
        if layer < N_A_LAYERS:
            proj = h @ w_in_a[layer]
            q = proj[..., :RET_QK_WIDTH].reshape(b, s, RET_HEADS, RET_QK_DIM)
            k = proj[..., RET_QK_WIDTH:2 * RET_QK_WIDTH].reshape(b, s, RET_HEADS, RET_QK_DIM)
            v = proj[..., 2 * RET_QK_WIDTH:2 * RET_QK_WIDTH + RET_WIDTH].reshape(b, s, RET_HEADS, RET_V_DIM)
            gate = proj[..., 2 * RET_QK_WIDTH + RET_WIDTH:]
            o = retention(rope(q), rope(k), v).reshape(b, s, RET_WIDTH)
            y = (o.astype(x.dtype) * jax.nn.silu(gate)) @ w_out_a[layer]
        else:
            j = layer - N_A_LAYERS
            if k_shared is None:
                kv = rmsnorm(x, kv_norm) @ w_kv
                k_shared = rope(kv[..., :DIFF_QK_WIDTH].reshape(b, s, DIFF_HEADS, 2, DIFF_HEAD_DIM))
                v_shared = kv[..., DIFF_QK_WIDTH:].reshape(b, s, DIFF_HEADS, DIFF_V_DIM)
            lambda_init = 0.8 - 0.6 * math.exp(-0.3 * layer)
            lam = (jnp.exp(jnp.sum(lam_q1[j].astype(jnp.float32) * lam_k1[j].astype(jnp.float32)))
                   - jnp.exp(jnp.sum(lam_q2[j].astype(jnp.float32) * lam_k2[j].astype(jnp.float32)))
                   + lambda_init)
            proj = h @ w_in_b[layer - N_A_LAYERS]
            q = rope(proj[..., :DIFF_QK_WIDTH].reshape(b, s, DIFF_HEADS, 2, DIFF_HEAD_DIM))
            gate = proj[..., DIFF_QK_WIDTH:]
            o = diff_attention(q, k_shared, v_shared, lam)
            o = rmsnorm(o, sub_norm_b[j]) * (1.0 - lambda_init)
            o = o.reshape(b, s, DIFF_WIDTH).astype(x.dtype)
            y = (o * jax.nn.silu(gate)) @ w_out_b[j]
        x = x + rmsnorm(y, post_norm[layer])
    return x
```

```python
import functools
import math

import jax
import jax.numpy as jnp
from jax import lax
from jax.experimental import pallas as pl
from jax.experimental.pallas import tpu as pltpu

F32 = jnp.float32
BF16 = jnp.bfloat16

D_MODEL = 1024
DEPTH = 4
N_A_LAYERS = DEPTH // 2
CHUNK = 64
EPS = 1e-6
ROPE_THETA = 10000.0

RET_HEADS = 8
RET_QK_DIM = 128
RET_V_DIM = 256
RET_QK_WIDTH = RET_HEADS * RET_QK_DIM
RET_WIDTH = RET_HEADS * RET_V_DIM
RET_IN_WIDTH = 2 * RET_QK_WIDTH + 2 * RET_WIDTH

DIFF_HEADS = 8
DIFF_HEAD_DIM = 64
DIFF_V_DIM = 128
DIFF_QK_WIDTH = DIFF_HEADS * 2 * DIFF_HEAD_DIM
DIFF_WIDTH = DIFF_HEADS * DIFF_V_DIM

LANES = 128
VMEM_LIMIT = 56 * 1024 * 1024

PROJ_TM = 512
RET_BLOCK = 512
RET_CHUNK = 256
ATT_TQ = 256
ATT_TK = 256
MASK_NEG = -1e30

NT_DIMS = (((1,), (1,)), ((), ()))
TN_DIMS = (((0,), (0,)), ((), ()))


def _params(sem):
    return pltpu.CompilerParams(dimension_semantics=sem, vmem_limit_bytes=VMEM_LIMIT)


def _normed(x_ref, g_ref):
    x = x_ref[...]
    ms = jnp.mean(x * x, axis=-1, keepdims=True)
    return (x * lax.rsqrt(ms + EPS) * g_ref[...]).astype(BF16)


def _silu(g):
    return g / (1.0 + jnp.exp(-g))


def _ret_in_proj_kernel(x_ref, g_ref, w_ref, cos_ref, sin_ref, o_ref):
    h = _normed(x_ref, g_ref)
    cos = cos_ref[...]
    sin = sin_ref[...]
    k_scale = RET_QK_DIM ** -0.5
    for j in range(RET_IN_WIDTH // D_MODEL):
        cols = slice(j * D_MODEL, (j + 1) * D_MODEL)
        r = jnp.dot(h, w_ref[:, cols], preferred_element_type=F32)
        if j < 2:
            for s in range(RET_HEADS):
                xs = r[:, s * LANES:(s + 1) * LANES]
                y = xs * cos + pltpu.roll(xs, RET_QK_DIM // 2, 1) * sin
                if j == 1:
                    y = y * k_scale
                o_ref[:, j * D_MODEL + s * LANES:j * D_MODEL + (s + 1) * LANES] = y.astype(BF16)
        else:
            o_ref[:, cols] = r.astype(BF16)


def _retention_kernel(q_ref, k_ref, v_ref, gate_ref, dm_ref, qd_ref, kd_ref, o_ref, state_ref):
    @pl.when(pl.program_id(2) == 0)
    def _():
        state_ref[...] = jnp.zeros_like(state_ref)

    dm = dm_ref[0]
    qd = qd_ref[0]
    kd = kd_ref[0]
    cdec = qd[RET_CHUNK - 1:RET_CHUNK, :]
    for i in range(RET_BLOCK // RET_CHUNK):
        rows = slice(i * RET_CHUNK, (i + 1) * RET_CHUNK)
        q = q_ref[rows, :]
        k = k_ref[rows, :]
        v = v_ref[rows, :]
        s = lax.dot_general(q, k, NT_DIMS, preferred_element_type=F32) * dm
        st = state_ref[...]
        o = jnp.dot(s.astype(BF16), v, preferred_element_type=F32)
        o = o + qd * jnp.dot(q, st.astype(BF16), preferred_element_type=F32)
        kdk = (k.astype(F32) * kd).astype(BF16)
        state_ref[...] = st * cdec + lax.dot_general(kdk, v, TN_DIMS, preferred_element_type=F32)
        mu = jnp.mean(o, axis=-1, keepdims=True)
        d = o - mu
        var = jnp.mean(d * d, axis=-1, keepdims=True)
        on = d * lax.rsqrt(var + EPS)
        o_ref[rows, :] = (on * _silu(gate_ref[rows, :].astype(F32))).astype(BF16)


def _out_proj_kernel(o_ref, w_ref, x_ref, g_ref, xo_ref):
    y = jnp.dot(o_ref[...], w_ref[...], preferred_element_type=F32)
    ms = jnp.mean(y * y, axis=-1, keepdims=True)
    xo_ref[...] = x_ref[...] + y * lax.rsqrt(ms + EPS) * g_ref[...]


def _retention_tables():
    h = jnp.arange(RET_HEADS, dtype=F32)
    log_gamma = jnp.log1p(-jnp.exp2(-5.0 - h))
    pos = jnp.arange(RET_CHUNK, dtype=F32)
    diff = pos[:, None] - pos[None, :]
    dm = jnp.where(diff[None] >= 0,
                   jnp.exp(jnp.maximum(diff, 0.0)[None] * log_gamma[:, None, None]), 0.0)
    qd = jnp.exp((pos[None, :] + 1.0) * log_gamma[:, None])
    kd = jnp.exp((RET_CHUNK - 1.0 - pos[None, :]) * log_gamma[:, None])
    qd = jnp.broadcast_to(qd[:, :, None], (RET_HEADS, RET_CHUNK, RET_V_DIM))
    kd = jnp.broadcast_to(kd[:, :, None], (RET_HEADS, RET_CHUNK, RET_QK_DIM))
    return dm, qd, kd


def _rope_tables_rows(seq, dim):
    half = dim // 2
    inv = ROPE_THETA ** (-jnp.arange(half, dtype=F32) / half)
    ang = jnp.arange(seq, dtype=F32)[:, None] * inv[None, :]
    cos = jnp.cos(ang)
    sin = jnp.sin(ang)
    reps = LANES // dim
    cos_full = jnp.tile(jnp.concatenate([cos, cos], axis=1), (1, reps))
    sin_signed = jnp.tile(jnp.concatenate([-sin, sin], axis=1), (1, reps))
    return cos_full, sin_signed


def _retention_layer(x2, batch, seq, pre_g, post_g, w_in, w_out, rope_tabs, ret_tabs):
    tokens = x2.shape[0]
    cos, sin = rope_tabs
    n_seq_tiles = seq // PROJ_TM
    proj = pl.pallas_call(
        _ret_in_proj_kernel,
        out_shape=jax.ShapeDtypeStruct((tokens, RET_IN_WIDTH), BF16),
        grid=(tokens // PROJ_TM,),
        in_specs=[
            pl.BlockSpec((PROJ_TM, D_MODEL), lambda i: (i, 0)),
            pl.BlockSpec((1, D_MODEL), lambda i: (0, 0)),
            pl.BlockSpec((D_MODEL, RET_IN_WIDTH), lambda i: (0, 0)),
            pl.BlockSpec((PROJ_TM, LANES), lambda i: (i % n_seq_tiles, 0)),
            pl.BlockSpec((PROJ_TM, LANES), lambda i: (i % n_seq_tiles, 0)),
        ],
        out_specs=pl.BlockSpec((PROJ_TM, RET_IN_WIDTH), lambda i: (i, 0)),
        compiler_params=_params(("arbitrary",)),
        name="ret_in_proj",
    )(x2, pre_g, w_in, cos, sin)

    dm, qd, kd = ret_tabs
    nblk = seq // RET_BLOCK
    v_off = 2 * RET_QK_WIDTH // RET_V_DIM
    g_off = (2 * RET_QK_WIDTH + RET_WIDTH) // RET_V_DIM
    o = pl.pallas_call(
        _retention_kernel,
        out_shape=jax.ShapeDtypeStruct((tokens, RET_WIDTH), BF16),
        grid=(batch, RET_HEADS, nblk),
        in_specs=[
            pl.BlockSpec((RET_BLOCK, RET_QK_DIM), lambda b, h, c: (b * nblk + c, h)),
            pl.BlockSpec((RET_BLOCK, RET_QK_DIM), lambda b, h, c: (b * nblk + c, RET_HEADS + h)),
            pl.BlockSpec((RET_BLOCK, RET_V_DIM), lambda b, h, c: (b * nblk + c, v_off + h)),
            pl.BlockSpec((RET_BLOCK, RET_V_DIM), lambda b, h, c: (b * nblk + c, g_off + h)),
            pl.BlockSpec((1, RET_CHUNK, RET_CHUNK), lambda b, h, c: (h, 0, 0)),
            pl.BlockSpec((1, RET_CHUNK, RET_V_DIM), lambda b, h, c: (h, 0, 0)),
            pl.BlockSpec((1, RET_CHUNK, RET_QK_DIM), lambda b, h, c: (h, 0, 0)),
        ],
        out_specs=pl.BlockSpec((RET_BLOCK, RET_V_DIM), lambda b, h, c: (b * nblk + c, h)),
        scratch_shapes=[pltpu.VMEM((RET_QK_DIM, RET_V_DIM), F32)],
        compiler_params=_params(("arbitrary", "arbitrary", "arbitrary")),
        name="retention",
    )(proj, proj, proj, proj, dm, qd, kd)

    return pl.pallas_call(
        _out_proj_kernel,
        out_shape=jax.ShapeDtypeStruct((tokens, D_MODEL), F32),
        grid=(tokens // PROJ_TM,),
        in_specs=[
            pl.BlockSpec((PROJ_TM, RET_WIDTH), lambda i: (i, 0)),
            pl.BlockSpec((RET_WIDTH, D_MODEL), lambda i: (0, 0)),
            pl.BlockSpec((PROJ_TM, D_MODEL), lambda i: (i, 0)),
            pl.BlockSpec((1, D_MODEL), lambda i: (0, 0)),
        ],
        out_specs=pl.BlockSpec((PROJ_TM, D_MODEL), lambda i: (i, 0)),
        compiler_params=_params(("arbitrary",)),
        name="ret_out_proj",
    )(o, w_out, x2, post_g)


def _kv_proj_kernel(x_ref, g_ref, wk_ref, wvt_ref, cos_ref, sin_ref, k_ref, vt_ref):
    h = _normed(x_ref, g_ref)
    cos = cos_ref[...]
    sin = sin_ref[...]
    kf = jnp.dot(h, wk_ref[...], preferred_element_type=F32)
    lane = lax.broadcasted_iota(jnp.int32, (PROJ_TM, LANES), 1)
    first_half = (lane & (DIFF_HEAD_DIM - 1)) < (DIFF_HEAD_DIM // 2)
    for s in range(DIFF_QK_WIDTH // LANES):
        xs = kf[:, s * LANES:(s + 1) * LANES]
        rot = jnp.where(first_half,
                        pltpu.roll(xs, LANES - DIFF_HEAD_DIM // 2, 1),
                        pltpu.roll(xs, DIFF_HEAD_DIM // 2, 1))
        k_ref[:, s * LANES:(s + 1) * LANES] = (xs * cos + rot * sin).astype(BF16)
    vt = lax.dot_general(wvt_ref[...], h, NT_DIMS, preferred_element_type=F32)
    vt_ref[0] = vt.astype(BF16)


def _q_proj_kernel(x_ref, g_ref, wt_ref, cos_ref, sin_ref, qt_ref, gt_ref):
    h = _normed(x_ref, g_ref)
    pt = lax.dot_general(wt_ref[...], h, NT_DIMS, preferred_element_type=F32)
    cos = cos_ref[...]
    sin = sin_ref[...]
    scale = DIFF_HEAD_DIM ** -0.5 * math.log2(math.e)
    half = DIFF_HEAD_DIM // 2
    for grp in range(DIFF_QK_WIDTH // DIFF_HEAD_DIM):
        r0 = grp * DIFF_HEAD_DIM
        x1 = pt[r0:r0 + half, :]
        x2 = pt[r0 + half:r0 + DIFF_HEAD_DIM, :]
        qt_ref[0, r0:r0 + half, :] = ((x1 * cos - x2 * sin) * scale).astype(BF16)
        qt_ref[0, r0 + half:r0 + DIFF_HEAD_DIM, :] = ((x1 * sin + x2 * cos) * scale).astype(BF16)
    gt_ref[0] = pt[DIFF_QK_WIDTH:, :].astype(BF16)


def _diff_attn_kernel(qt_ref, k_ref, vt_ref, gt_ref, sub_ref, lq1_ref, lk1_ref, lq2_ref, lk2_ref,
                      ot_ref, qbd_ref, acc_ref, m_ref, l_ref, *, lambda_init):
    qi = pl.program_id(2)
    tq, tk = ATT_TQ, ATT_TK
    dh = DIFF_HEAD_DIM
    qt = qt_ref[0]
    zeros = jnp.zeros((dh, tq), BF16)
    qbd_ref[0:dh, 0:tq] = qt[0:dh, :]
    qbd_ref[0:dh, tq:2 * tq] = zeros
    qbd_ref[dh:2 * dh, 0:tq] = zeros
    qbd_ref[dh:2 * dh, tq:2 * tq] = qt[dh:2 * dh, :]

    k0 = pl.multiple_of(qi * tq, tq)
    s = jnp.dot(k_ref[0, pl.ds(k0, tk), :], qbd_ref[...], preferred_element_type=F32)
    key_chunk = lax.broadcasted_iota(jnp.int32, (tk, 2 * tq), 0) // CHUNK
    qry_chunk = (lax.broadcasted_iota(jnp.int32, (tk, 2 * tq), 1) % tq) // CHUNK
    s = jnp.where(key_chunk <= qry_chunk, s, MASK_NEG)
    m0 = jnp.max(s, axis=0, keepdims=True)
    p = jnp.exp2(s - m0)
    m_ref[...] = m0
    l_ref[...] = jnp.sum(p, axis=0, keepdims=True)
    acc_ref[...] = jnp.dot(vt_ref[0, :, pl.ds(k0, tk)], p.astype(BF16), preferred_element_type=F32)

    def body(j, carry):
        kk = pl.multiple_of(j * tk, tk)
        sj = jnp.dot(k_ref[0, pl.ds(kk, tk), :], qbd_ref[...], preferred_element_type=F32)
        m_old = m_ref[...]
        m_new = jnp.maximum(m_old, jnp.max(sj, axis=0, keepdims=True))
        alpha = jnp.exp2(m_old - m_new)
        pj = jnp.exp2(sj - m_new)
        l_ref[...] = alpha * l_ref[...] + jnp.sum(pj, axis=0, keepdims=True)
        pv = jnp.dot(vt_ref[0, :, pl.ds(kk, tk)], pj.astype(BF16), preferred_element_type=F32)
        acc_ref[...] = alpha * acc_ref[...] + pv
        m_ref[...] = m_new
        return carry

    lax.fori_loop(0, qi, body, 0)

    a = acc_ref[...] * (1.0 / l_ref[...])
    lam = (jnp.exp(jnp.sum(lq1_ref[...] * lk1_ref[...], axis=-1, keepdims=True))
           - jnp.exp(jnp.sum(lq2_ref[...] * lk2_ref[...], axis=-1, keepdims=True))
           + lambda_init)
    o = a[:, 0:tq] - lam * a[:, tq:2 * tq]
    ms = jnp.mean(o * o, axis=0, keepdims=True)
    on = o * lax.rsqrt(ms + EPS) * sub_ref[...] * (1.0 - lambda_init)
    ot_ref[0] = (on * _silu(gt_ref[0].astype(F32))).astype(BF16)


def _out_proj_t_kernel(ot_ref, w_ref, x_ref, g_ref, xo_ref):
    y = lax.dot_general(ot_ref[0], w_ref[...], TN_DIMS, preferred_element_type=F32)
    ms = jnp.mean(y * y, axis=-1, keepdims=True)
    xo_ref[...] = x_ref[...] + y * lax.rsqrt(ms + EPS) * g_ref[...]


def _kv_proj(x2, batch, seq, kv_g, wk, wvt, rope_tabs):
    tokens = x2.shape[0]
    cos, sin = rope_tabs
    nst = seq // PROJ_TM
    return pl.pallas_call(
        _kv_proj_kernel,
        out_shape=(jax.ShapeDtypeStruct((tokens, DIFF_QK_WIDTH), BF16),
                   jax.ShapeDtypeStruct((batch, DIFF_WIDTH, seq), BF16)),
        grid=(batch, nst),
        in_specs=[
            pl.BlockSpec((PROJ_TM, D_MODEL), lambda b, i: (b * nst + i, 0)),
            pl.BlockSpec((1, D_MODEL), lambda b, i: (0, 0)),
            pl.BlockSpec((D_MODEL, DIFF_QK_WIDTH), lambda b, i: (0, 0)),
            pl.BlockSpec((DIFF_WIDTH, D_MODEL), lambda b, i: (0, 0)),
            pl.BlockSpec((PROJ_TM, LANES), lambda b, i: (i, 0)),
            pl.BlockSpec((PROJ_TM, LANES), lambda b, i: (i, 0)),
        ],
        out_specs=(pl.BlockSpec((PROJ_TM, DIFF_QK_WIDTH), lambda b, i: (b * nst + i, 0)),
                   pl.BlockSpec((1, DIFF_WIDTH, PROJ_TM), lambda b, i: (b, 0, i))),
        compiler_params=_params(("arbitrary", "arbitrary")),
        name="kv_proj",
    )(x2, kv_g, wk, wvt, cos, sin)


def _diff_layer(x2, batch, seq, layer, pre_g, post_g, wt_in, w_out, k_sh, vt_sh, rope_t_tabs,
                sub_g, lq1, lk1, lq2, lk2):
    tokens = x2.shape[0]
    cos_t, sin_t = rope_t_tabs
    nst = seq // PROJ_TM
    qt, gt = pl.pallas_call(
        _q_proj_kernel,
        out_shape=(jax.ShapeDtypeStruct((batch, DIFF_QK_WIDTH, seq), BF16),
                   jax.ShapeDtypeStruct((batch, DIFF_WIDTH, seq), BF16)),
        grid=(batch, nst),
        in_specs=[
            pl.BlockSpec((PROJ_TM, D_MODEL), lambda b, i: (b * nst + i, 0)),
            pl.BlockSpec((1, D_MODEL), lambda b, i: (0, 0)),
            pl.BlockSpec((DIFF_QK_WIDTH + DIFF_WIDTH, D_MODEL), lambda b, i: (0, 0)),
            pl.BlockSpec((DIFF_HEAD_DIM // 2, PROJ_TM), lambda b, i: (0, i)),
            pl.BlockSpec((DIFF_HEAD_DIM // 2, PROJ_TM), lambda b, i: (0, i)),
        ],
        out_specs=(pl.BlockSpec((1, DIFF_QK_WIDTH, PROJ_TM), lambda b, i: (b, 0, i)),
                   pl.BlockSpec((1, DIFF_WIDTH, PROJ_TM), lambda b, i: (b, 0, i))),
        compiler_params=_params(("arbitrary", "arbitrary")),
        name="q_proj",
    )(x2, pre_g, wt_in, cos_t, sin_t)

    lambda_init = 0.8 - 0.6 * math.exp(-0.3 * layer)
    nq = seq // ATT_TQ
    k3 = k_sh.reshape(batch, seq, DIFF_QK_WIDTH)
    lam_spec = pl.BlockSpec((1, DIFF_HEAD_DIM), lambda b, h, q: (0, 0))
    ot = pl.pallas_call(
        functools.partial(_diff_attn_kernel, lambda_init=lambda_init),
        out_shape=jax.ShapeDtypeStruct((batch, DIFF_WIDTH, seq), BF16),
        grid=(batch, DIFF_HEADS, nq),
        in_specs=[
            pl.BlockSpec((1, 2 * DIFF_HEAD_DIM, ATT_TQ), lambda b, h, q: (b, h, q)),
            pl.BlockSpec((1, seq, 2 * DIFF_HEAD_DIM), lambda b, h, q: (b, 0, h)),
            pl.BlockSpec((1, DIFF_V_DIM, seq), lambda b, h, q: (b, h, 0)),
            pl.BlockSpec((1, DIFF_V_DIM, ATT_TQ), lambda b, h, q: (b, h, q)),
            pl.BlockSpec((DIFF_V_DIM, 1), lambda b, h, q: (0, 0)),
            lam_spec, lam_spec, lam_spec, lam_spec,
        ],
        out_specs=pl.BlockSpec((1, DIFF_V_DIM, ATT_TQ), lambda b, h, q: (b, h, q)),
        scratch_shapes=[
            pltpu.VMEM((2 * DIFF_HEAD_DIM, 2 * ATT_TQ), BF16),
            pltpu.VMEM((DIFF_V_DIM, 2 * ATT_TQ), F32),
            pltpu.VMEM((1, 2 * ATT_TQ), F32),
            pltpu.VMEM((1, 2 * ATT_TQ), F32),
        ],
        compiler_params=_params(("arbitrary", "arbitrary", "arbitrary")),
        name="diff_attn",
    )(qt, k3, vt_sh, gt, sub_g, lq1, lk1, lq2, lk2)

    return pl.pallas_call(
        _out_proj_t_kernel,
        out_shape=jax.ShapeDtypeStruct((tokens, D_MODEL), F32),
        grid=(batch, nst),
        in_specs=[
            pl.BlockSpec((1, DIFF_WIDTH, PROJ_TM), lambda b, i: (b, 0, i)),
            pl.BlockSpec((DIFF_WIDTH, D_MODEL), lambda b, i: (0, 0)),
            pl.BlockSpec((PROJ_TM, D_MODEL), lambda b, i: (b * nst + i, 0)),
            pl.BlockSpec((1, D_MODEL), lambda b, i: (0, 0)),
        ],
        out_specs=pl.BlockSpec((PROJ_TM, D_MODEL), lambda b, i: (b * nst + i, 0)),
        compiler_params=_params(("arbitrary", "arbitrary")),
        name="diff_out_proj",
    )(ot, w_out, x2, post_g)


def kernel(x, pre_norm, post_norm, w_in_a, w_out_a, kv_norm, w_kv, w_in_b,
           lam_q1, lam_k1, lam_q2, lam_k2, sub_norm_b, w_out_b):
    batch, seq, _ = x.shape
    x2 = x.reshape(batch * seq, D_MODEL)

    ret_rope = _rope_tables_rows(seq, RET_QK_DIM)
    ret_tabs = _retention_tables()
    for layer in range(N_A_LAYERS):
        x2 = _retention_layer(
            x2, batch, seq,
            pre_norm[layer].reshape(1, D_MODEL), post_norm[layer].reshape(1, D_MODEL),
            w_in_a[layer].astype(BF16), w_out_a[layer].astype(BF16), ret_rope, ret_tabs)

    diff_rope = _rope_tables_rows(seq, DIFF_HEAD_DIM)
    k_sh, vt_sh = _kv_proj(
        x2, batch, seq, kv_norm.reshape(1, D_MODEL),
        w_kv[:, :DIFF_QK_WIDTH].astype(BF16), w_kv[:, DIFF_QK_WIDTH:].T.astype(BF16), diff_rope)

    half = DIFF_HEAD_DIM // 2
    inv = ROPE_THETA ** (-jnp.arange(half, dtype=F32) / half)
    ang_t = inv[:, None] * jnp.arange(seq, dtype=F32)[None, :]
    rope_t = (jnp.cos(ang_t), jnp.sin(ang_t))
    for layer in range(N_A_LAYERS, DEPTH):
        j = layer - N_A_LAYERS
        x2 = _diff_layer(
            x2, batch, seq, layer,
            pre_norm[layer].reshape(1, D_MODEL), post_norm[layer].reshape(1, D_MODEL),
            w_in_b[j].T.astype(BF16), w_out_b[j].astype(BF16), k_sh, vt_sh, rope_t,
            sub_norm_b[j].reshape(DIFF_V_DIM, 1),
            lam_q1[j].reshape(1, DIFF_HEAD_DIM), lam_k1[j].reshape(1, DIFF_HEAD_DIM),
            lam_q2[j].reshape(1, DIFF_HEAD_DIM), lam_k2[j].reshape(1, DIFF_HEAD_DIM))
    return x2.reshape(batch, seq, D_MODEL)
```

```python
import functools
import math

import jax
import jax.numpy as jnp
from jax import lax
from jax.experimental import pallas as pl
from jax.experimental.pallas import tpu as pltpu

F32 = jnp.float32
BF16 = jnp.bfloat16

D_MODEL = 1024
DEPTH = 4
N_A_LAYERS = DEPTH // 2
CHUNK = 64
EPS = 1e-6
ROPE_THETA = 10000.0

RET_HEADS = 8
RET_QK_DIM = 128
RET_V_DIM = 256
RET_QK_WIDTH = RET_HEADS * RET_QK_DIM
RET_WIDTH = RET_HEADS * RET_V_DIM
RET_IN_WIDTH = 2 * RET_QK_WIDTH + 2 * RET_WIDTH

DIFF_HEADS = 8
DIFF_HEAD_DIM = 64
DIFF_V_DIM = 128
DIFF_QK_WIDTH = DIFF_HEADS * 2 * DIFF_HEAD_DIM
DIFF_WIDTH = DIFF_HEADS * DIFF_V_DIM

LANES = 128
VMEM_LIMIT = 56 * 1024 * 1024

PROJ_TM = 512
RET_BLOCK = 512
RET_CHUNK = 256
ATT_TQ = 512
ATT_TK = 512
MASK_NEG = -1e30

NT_DIMS = (((1,), (1,)), ((), ()))
TN_DIMS = (((0,), (0,)), ((), ()))


def _params(sem):
    return pltpu.CompilerParams(dimension_semantics=sem, vmem_limit_bytes=VMEM_LIMIT)


def _normed(x_ref, g_ref):
    x = x_ref[...]
    ms = jnp.mean(x * x, axis=-1, keepdims=True)
    return (x * lax.rsqrt(ms + EPS) * g_ref[...]).astype(BF16)


def _silu(g):
    return g / (1.0 + jnp.exp(-g))


def _ret_in_proj_kernel(x_ref, g_ref, w_ref, cos_ref, sin_ref, o_ref):
    h = _normed(x_ref, g_ref)
    cos = cos_ref[...]
    sin = sin_ref[...]
    k_scale = RET_QK_DIM ** -0.5
    for j in range(RET_IN_WIDTH // D_MODEL):
        cols = slice(j * D_MODEL, (j + 1) * D_MODEL)
        r = jnp.dot(h, w_ref[:, cols], preferred_element_type=F32)
        if j < 2:
            for s in range(RET_HEADS):
                xs = r[:, s * LANES:(s + 1) * LANES]
                y = xs * cos + pltpu.roll(xs, RET_QK_DIM // 2, 1) * sin
                if j == 1:
                    y = y * k_scale
                o_ref[:, j * D_MODEL + s * LANES:j * D_MODEL + (s + 1) * LANES] = y.astype(BF16)
        else:
            o_ref[:, cols] = r.astype(BF16)


def _retention_kernel(q_ref, k_ref, v_ref, gate_ref, dm_ref, qd_ref, kd_ref, o_ref, state_ref):
    @pl.when(pl.program_id(2) == 0)
    def _():
        state_ref[...] = jnp.zeros_like(state_ref)

    dm = dm_ref[0]
    qd = qd_ref[0]
    kd = kd_ref[0]
    cdec = qd[RET_CHUNK - 1:RET_CHUNK, :]
    for i in range(RET_BLOCK // RET_CHUNK):
        rows = slice(i * RET_CHUNK, (i + 1) * RET_CHUNK)
        q = q_ref[rows, :]
        k = k_ref[rows, :]
        v = v_ref[rows, :]
        s = lax.dot_general(q, k, NT_DIMS, preferred_element_type=F32) * dm
        st = state_ref[...]
        o = jnp.dot(s.astype(BF16), v, preferred_element_type=F32)
        o = o + qd * jnp.dot(q, st.astype(BF16), preferred_element_type=F32)
        kdk = (k.astype(F32) * kd).astype(BF16)
        state_ref[...] = st * cdec + lax.dot_general(kdk, v, TN_DIMS, preferred_element_type=F32)
        mu = jnp.mean(o, axis=-1, keepdims=True)
        d = o - mu
        var = jnp.mean(d * d, axis=-1, keepdims=True)
        on = d * lax.rsqrt(var + EPS)
        o_ref[rows, :] = (on * _silu(gate_ref[rows, :].astype(F32))).astype(BF16)


def _out_proj_kernel(o_ref, w_ref, x_ref, g_ref, xo_ref):
    y = jnp.dot(o_ref[...], w_ref[...], preferred_element_type=F32)
    ms = jnp.mean(y * y, axis=-1, keepdims=True)
    xo_ref[...] = x_ref[...] + y * lax.rsqrt(ms + EPS) * g_ref[...]


def _retention_tables():
    h = jnp.arange(RET_HEADS, dtype=F32)
    log_gamma = jnp.log1p(-jnp.exp2(-5.0 - h))
    pos = jnp.arange(RET_CHUNK, dtype=F32)
    diff = pos[:, None] - pos[None, :]
    dm = jnp.where(diff[None] >= 0,
                   jnp.exp(jnp.maximum(diff, 0.0)[None] * log_gamma[:, None, None]), 0.0)
    qd = jnp.exp((pos[None, :] + 1.0) * log_gamma[:, None])
    kd = jnp.exp((RET_CHUNK - 1.0 - pos[None, :]) * log_gamma[:, None])
    qd = jnp.broadcast_to(qd[:, :, None], (RET_HEADS, RET_CHUNK, RET_V_DIM))
    kd = jnp.broadcast_to(kd[:, :, None], (RET_HEADS, RET_CHUNK, RET_QK_DIM))
    return dm, qd, kd


def _rope_tables_rows(seq, dim):
    half = dim // 2
    inv = ROPE_THETA ** (-jnp.arange(half, dtype=F32) / half)
    ang = jnp.arange(seq, dtype=F32)[:, None] * inv[None, :]
    cos = jnp.cos(ang)
    sin = jnp.sin(ang)
    reps = LANES // dim
    cos_full = jnp.tile(jnp.concatenate([cos, cos], axis=1), (1, reps))
    sin_signed = jnp.tile(jnp.concatenate([-sin, sin], axis=1), (1, reps))
    return cos_full, sin_signed


def _retention_layer(x2, batch, seq, pre_g, post_g, w_in, w_out, rope_tabs, ret_tabs):
    tokens = x2.shape[0]
    cos, sin = rope_tabs
    n_seq_tiles = seq // PROJ_TM
    proj = pl.pallas_call(
        _ret_in_proj_kernel,
        out_shape=jax.ShapeDtypeStruct((tokens, RET_IN_WIDTH), BF16),
        grid=(tokens // PROJ_TM,),
        in_specs=[
            pl.BlockSpec((PROJ_TM, D_MODEL), lambda i: (i, 0)),
            pl.BlockSpec((1, D_MODEL), lambda i: (0, 0)),
            pl.BlockSpec((D_MODEL, RET_IN_WIDTH), lambda i: (0, 0)),
            pl.BlockSpec((PROJ_TM, LANES), lambda i: (i % n_seq_tiles, 0)),
            pl.BlockSpec((PROJ_TM, LANES), lambda i: (i % n_seq_tiles, 0)),
        ],
        out_specs=pl.BlockSpec((PROJ_TM, RET_IN_WIDTH), lambda i: (i, 0)),
        compiler_params=_params(("arbitrary",)),
        name="ret_in_proj",
    )(x2, pre_g, w_in, cos, sin)

    dm, qd, kd = ret_tabs
    nblk = seq // RET_BLOCK
    v_off = 2 * RET_QK_WIDTH // RET_V_DIM
    g_off = (2 * RET_QK_WIDTH + RET_WIDTH) // RET_V_DIM
    o = pl.pallas_call(
        _retention_kernel,
        out_shape=jax.ShapeDtypeStruct((tokens, RET_WIDTH), BF16),
        grid=(batch, RET_HEADS, nblk),
        in_specs=[
            pl.BlockSpec((RET_BLOCK, RET_QK_DIM), lambda b, h, c: (b * nblk + c, h)),
            pl.BlockSpec((RET_BLOCK, RET_QK_DIM), lambda b, h, c: (b * nblk + c, RET_HEADS + h)),
            pl.BlockSpec((RET_BLOCK, RET_V_DIM), lambda b, h, c: (b * nblk + c, v_off + h)),
            pl.BlockSpec((RET_BLOCK, RET_V_DIM), lambda b, h, c: (b * nblk + c, g_off + h)),
            pl.BlockSpec((1, RET_CHUNK, RET_CHUNK), lambda b, h, c: (h, 0, 0)),
            pl.BlockSpec((1, RET_CHUNK, RET_V_DIM), lambda b, h, c: (h, 0, 0)),
            pl.BlockSpec((1, RET_CHUNK, RET_QK_DIM), lambda b, h, c: (h, 0, 0)),
        ],
        out_specs=pl.BlockSpec((RET_BLOCK, RET_V_DIM), lambda b, h, c: (b * nblk + c, h)),
        scratch_shapes=[pltpu.VMEM((RET_QK_DIM, RET_V_DIM), F32)],
        compiler_params=_params(("arbitrary", "arbitrary", "arbitrary")),
        name="retention",
    )(proj, proj, proj, proj, dm, qd, kd)

    return pl.pallas_call(
        _out_proj_kernel,
        out_shape=jax.ShapeDtypeStruct((tokens, D_MODEL), F32),
        grid=(tokens // PROJ_TM,),
        in_specs=[
            pl.BlockSpec((PROJ_TM, RET_WIDTH), lambda i: (i, 0)),
            pl.BlockSpec((RET_WIDTH, D_MODEL), lambda i: (0, 0)),
            pl.BlockSpec((PROJ_TM, D_MODEL), lambda i: (i, 0)),
            pl.BlockSpec((1, D_MODEL), lambda i: (0, 0)),
        ],
        out_specs=pl.BlockSpec((PROJ_TM, D_MODEL), lambda i: (i, 0)),
        compiler_params=_params(("arbitrary",)),
        name="ret_out_proj",
    )(o, w_out, x2, post_g)


def _kv_proj_kernel(x_ref, g_ref, wk_ref, wvt_ref, cos_ref, sin_ref, k_ref, vt_ref):
    h = _normed(x_ref, g_ref)
    cos = cos_ref[...]
    sin = sin_ref[...]
    kf = jnp.dot(h, wk_ref[...], preferred_element_type=F32)
    lane = lax.broadcasted_iota(jnp.int32, (PROJ_TM, LANES), 1)
    first_half = (lane & (DIFF_HEAD_DIM - 1)) < (DIFF_HEAD_DIM // 2)
    for s in range(DIFF_QK_WIDTH // LANES):
        xs = kf[:, s * LANES:(s + 1) * LANES]
        rot = jnp.where(first_half,
                        pltpu.roll(xs, LANES - DIFF_HEAD_DIM // 2, 1),
                        pltpu.roll(xs, DIFF_HEAD_DIM // 2, 1))
        k_ref[:, s * LANES:(s + 1) * LANES] = (xs * cos + rot * sin).astype(BF16)
    vt = lax.dot_general(wvt_ref[...], h, NT_DIMS, preferred_element_type=F32)
    vt_ref[0] = vt.astype(BF16)


def _q_proj_kernel(x_ref, g_ref, wt_ref, cos_ref, sin_ref, qt_ref, gt_ref):
    h = _normed(x_ref, g_ref)
    pt = lax.dot_general(wt_ref[...], h, NT_DIMS, preferred_element_type=F32)
    cos = cos_ref[...]
    sin = sin_ref[...]
    scale = DIFF_HEAD_DIM ** -0.5 * math.log2(math.e)
    half = DIFF_HEAD_DIM // 2
    for grp in range(DIFF_QK_WIDTH // DIFF_HEAD_DIM):
        r0 = grp * DIFF_HEAD_DIM
        x1 = pt[r0:r0 + half, :]
        x2 = pt[r0 + half:r0 + DIFF_HEAD_DIM, :]
        qt_ref[0, r0:r0 + half, :] = ((x1 * cos - x2 * sin) * scale).astype(BF16)
        qt_ref[0, r0 + half:r0 + DIFF_HEAD_DIM, :] = ((x1 * sin + x2 * cos) * scale).astype(BF16)
    gt_ref[0] = pt[DIFF_QK_WIDTH:, :].astype(BF16)


def _diff_attn_kernel(qt_ref, k_ref, vt_ref, gt_ref, sub_ref, lq1_ref, lk1_ref, lq2_ref, lk2_ref,
                      ot_ref, qbd_ref, acc_ref, m_ref, l_ref, sa_ref, sb_ref, *, lambda_init):
    qi = pl.program_id(2)
    tq, tk = ATT_TQ, ATT_TK
    dh = DIFF_HEAD_DIM
    qt = qt_ref[0]
    zeros = jnp.zeros((dh, tq), BF16)
    qbd_ref[0:dh, 0:tq] = qt[0:dh, :]
    qbd_ref[0:dh, tq:2 * tq] = zeros
    qbd_ref[dh:2 * dh, 0:tq] = zeros
    qbd_ref[dh:2 * dh, tq:2 * tq] = qt[dh:2 * dh, :]
    m_ref[...] = jnp.full(m_ref.shape, MASK_NEG, F32)
    l_ref[...] = jnp.zeros(l_ref.shape, F32)
    acc_ref[...] = jnp.zeros(acc_ref.shape, F32)

    def scores(s_ref, j):
        kk = pl.multiple_of(j * tk, tk)
        s_ref[...] = jnp.dot(k_ref[0, pl.ds(kk, tk), :], qbd_ref[...], preferred_element_type=F32)

    def consume(s_ref, j, diagonal):
        kk = pl.multiple_of(j * tk, tk)
        s = s_ref[...]
        if diagonal:
            key_chunk = lax.broadcasted_iota(jnp.int32, (tk, 2 * tq), 0) // CHUNK
            qry_chunk = (lax.broadcasted_iota(jnp.int32, (tk, 2 * tq), 1) % tq) // CHUNK
            s = jnp.where(key_chunk <= qry_chunk, s, MASK_NEG)
        m_old = m_ref[...]
        m_new = jnp.maximum(m_old, jnp.max(s, axis=0, keepdims=True))
        alpha = jnp.exp2(m_old - m_new)
        p = jnp.exp2(s - m_new)
        l_ref[...] = alpha * l_ref[...] + jnp.sum(p, axis=0, keepdims=True)
        pv = jnp.dot(vt_ref[0, :, pl.ds(kk, tk)], p.astype(BF16), preferred_element_type=F32)
        acc_ref[...] = alpha * acc_ref[...] + pv
        m_ref[...] = m_new

    scores(sa_ref, 0)

    def pair(pj, carry):
        j = 2 * pj
        scores(sb_ref, j + 1)
        consume(sa_ref, j, False)
        scores(sa_ref, j + 2)
        consume(sb_ref, j + 1, False)
        return carry

    lax.fori_loop(0, qi // 2, pair, 0)

    @pl.when(qi % 2 == 0)
    def _():
        consume(sa_ref, qi, True)

    @pl.when(qi % 2 == 1)
    def _():
        scores(sb_ref, qi)
        consume(sa_ref, qi - 1, False)
        consume(sb_ref, qi, True)

    a = acc_ref[...] * (1.0 / l_ref[...])
    lam = (jnp.exp(jnp.sum(lq1_ref[...] * lk1_ref[...], axis=-1, keepdims=True))
           - jnp.exp(jnp.sum(lq2_ref[...] * lk2_ref[...], axis=-1, keepdims=True))
           + lambda_init)
    o = a[:, 0:tq] - lam * a[:, tq:2 * tq]
    ms = jnp.mean(o * o, axis=0, keepdims=True)
    on = o * lax.rsqrt(ms + EPS) * sub_ref[...] * (1.0 - lambda_init)
    ot_ref[0] = (on * _silu(gt_ref[0].astype(F32))).astype(BF16)


def _out_proj_t_kernel(ot_ref, w_ref, x_ref, g_ref, xo_ref):
    y = lax.dot_general(ot_ref[0], w_ref[...], TN_DIMS, preferred_element_type=F32)
    ms = jnp.mean(y * y, axis=-1, keepdims=True)
    xo_ref[...] = x_ref[...] + y * lax.rsqrt(ms + EPS) * g_ref[...]


def _kv_proj(x2, batch, seq, kv_g, wk, wvt, rope_tabs):
    tokens = x2.shape[0]
    cos, sin = rope_tabs
    nst = seq // PROJ_TM
    return pl.pallas_call(
        _kv_proj_kernel,
        out_shape=(jax.ShapeDtypeStruct((tokens, DIFF_QK_WIDTH), BF16),
                   jax.ShapeDtypeStruct((batch, DIFF_WIDTH, seq), BF16)),
        grid=(batch, nst),
        in_specs=[
            pl.BlockSpec((PROJ_TM, D_MODEL), lambda b, i: (b * nst + i, 0)),
            pl.BlockSpec((1, D_MODEL), lambda b, i: (0, 0)),
            pl.BlockSpec((D_MODEL, DIFF_QK_WIDTH), lambda b, i: (0, 0)),
            pl.BlockSpec((DIFF_WIDTH, D_MODEL), lambda b, i: (0, 0)),
            pl.BlockSpec((PROJ_TM, LANES), lambda b, i: (i, 0)),
            pl.BlockSpec((PROJ_TM, LANES), lambda b, i: (i, 0)),
        ],
        out_specs=(pl.BlockSpec((PROJ_TM, DIFF_QK_WIDTH), lambda b, i: (b * nst + i, 0)),
                   pl.BlockSpec((1, DIFF_WIDTH, PROJ_TM), lambda b, i: (b, 0, i))),
        compiler_params=_params(("arbitrary", "arbitrary")),
        name="kv_proj",
    )(x2, kv_g, wk, wvt, cos, sin)


def _diff_layer(x2, batch, seq, layer, pre_g, post_g, wt_in, w_out, k_sh, vt_sh, rope_t_tabs,
                sub_g, lq1, lk1, lq2, lk2):
    tokens = x2.shape[0]
    cos_t, sin_t = rope_t_tabs
    nst = seq // PROJ_TM
    qt, gt = pl.pallas_call(
        _q_proj_kernel,
        out_shape=(jax.ShapeDtypeStruct((batch, DIFF_QK_WIDTH, seq), BF16),
                   jax.ShapeDtypeStruct((batch, DIFF_WIDTH, seq), BF16)),
        grid=(batch, nst),
        in_specs=[
            pl.BlockSpec((PROJ_TM, D_MODEL), lambda b, i: (b * nst + i, 0)),
            pl.BlockSpec((1, D_MODEL), lambda b, i: (0, 0)),
            pl.BlockSpec((DIFF_QK_WIDTH + DIFF_WIDTH, D_MODEL), lambda b, i: (0, 0)),
            pl.BlockSpec((DIFF_HEAD_DIM // 2, PROJ_TM), lambda b, i: (0, i)),
            pl.BlockSpec((DIFF_HEAD_DIM // 2, PROJ_TM), lambda b, i: (0, i)),
        ],
        out_specs=(pl.BlockSpec((1, DIFF_QK_WIDTH, PROJ_TM), lambda b, i: (b, 0, i)),
                   pl.BlockSpec((1, DIFF_WIDTH, PROJ_TM), lambda b, i: (b, 0, i))),
        compiler_params=_params(("arbitrary", "arbitrary")),
        name="q_proj",
    )(x2, pre_g, wt_in, cos_t, sin_t)

    lambda_init = 0.8 - 0.6 * math.exp(-0.3 * layer)
    nq = seq // ATT_TQ
    k3 = k_sh.reshape(batch, seq, DIFF_QK_WIDTH)
    lam_spec = pl.BlockSpec((1, DIFF_HEAD_DIM), lambda b, h, q: (0, 0))
    ot = pl.pallas_call(
        functools.partial(_diff_attn_kernel, lambda_init=lambda_init),
        out_shape=jax.ShapeDtypeStruct((batch, DIFF_WIDTH, seq), BF16),
        grid=(batch, DIFF_HEADS, nq),
        in_specs=[
            pl.BlockSpec((1, 2 * DIFF_HEAD_DIM, ATT_TQ), lambda b, h, q: (b, h, q)),
            pl.BlockSpec((1, seq, 2 * DIFF_HEAD_DIM), lambda b, h, q: (b, 0, h)),
            pl.BlockSpec((1, DIFF_V_DIM, seq), lambda b, h, q: (b, h, 0)),
            pl.BlockSpec((1, DIFF_V_DIM, ATT_TQ), lambda b, h, q: (b, h, q)),
            pl.BlockSpec((DIFF_V_DIM, 1), lambda b, h, q: (0, 0)),
            lam_spec, lam_spec, lam_spec, lam_spec,
        ],
        out_specs=pl.BlockSpec((1, DIFF_V_DIM, ATT_TQ), lambda b, h, q: (b, h, q)),
        scratch_shapes=[
            pltpu.VMEM((2 * DIFF_HEAD_DIM, 2 * ATT_TQ), BF16),
            pltpu.VMEM((DIFF_V_DIM, 2 * ATT_TQ), F32),
            pltpu.VMEM((1, 2 * ATT_TQ), F32),
            pltpu.VMEM((1, 2 * ATT_TQ), F32),
            pltpu.VMEM((ATT_TK, 2 * ATT_TQ), F32),
            pltpu.VMEM((ATT_TK, 2 * ATT_TQ), F32),
        ],
        compiler_params=_params(("arbitrary", "arbitrary", "arbitrary")),
        name="diff_attn",
    )(qt, k3, vt_sh, gt, sub_g, lq1, lk1, lq2, lk2)

    return pl.pallas_call(
        _out_proj_t_kernel,
        out_shape=jax.ShapeDtypeStruct((tokens, D_MODEL), F32),
        grid=(batch, nst),
        in_specs=[
            pl.BlockSpec((1, DIFF_WIDTH, PROJ_TM), lambda b, i: (b, 0, i)),
            pl.BlockSpec((DIFF_WIDTH, D_MODEL), lambda b, i: (0, 0)),
            pl.BlockSpec((PROJ_TM, D_MODEL), lambda b, i: (b * nst + i, 0)),
            pl.BlockSpec((1, D_MODEL), lambda b, i: (0, 0)),
        ],
        out_specs=pl.BlockSpec((PROJ_TM, D_MODEL), lambda b, i: (b * nst + i, 0)),
        compiler_params=_params(("arbitrary", "arbitrary")),
        name="diff_out_proj",
    )(ot, w_out, x2, post_g)


def kernel(x, pre_norm, post_norm, w_in_a, w_out_a, kv_norm, w_kv, w_in_b,
           lam_q1, lam_k1, lam_q2, lam_k2, sub_norm_b, w_out_b):
    batch, seq, _ = x.shape
    x2 = x.reshape(batch * seq, D_MODEL)

    ret_rope = _rope_tables_rows(seq, RET_QK_DIM)
    ret_tabs = _retention_tables()
    for layer in range(N_A_LAYERS):
        x2 = _retention_layer(
            x2, batch, seq,
            pre_norm[layer].reshape(1, D_MODEL), post_norm[layer].reshape(1, D_MODEL),
            w_in_a[layer].astype(BF16), w_out_a[layer].astype(BF16), ret_rope, ret_tabs)

    diff_rope = _rope_tables_rows(seq, DIFF_HEAD_DIM)
    k_sh, vt_sh = _kv_proj(
        x2, batch, seq, kv_norm.reshape(1, D_MODEL),
        w_kv[:, :DIFF_QK_WIDTH].astype(BF16), w_kv[:, DIFF_QK_WIDTH:].T.astype(BF16), diff_rope)

    half = DIFF_HEAD_DIM // 2
    inv = ROPE_THETA ** (-jnp.arange(half, dtype=F32) / half)
    ang_t = inv[:, None] * jnp.arange(seq, dtype=F32)[None, :]
    rope_t = (jnp.cos(ang_t), jnp.sin(ang_t))
    for layer in range(N_A_LAYERS, DEPTH):
        j = layer - N_A_LAYERS
        x2 = _diff_layer(
            x2, batch, seq, layer,
            pre_norm[layer].reshape(1, D_MODEL), post_norm[layer].reshape(1, D_MODEL),
            w_in_b[j].T.astype(BF16), w_out_b[j].astype(BF16), k_sh, vt_sh, rope_t,
            sub_norm_b[j].reshape(DIFF_V_DIM, 1),
            lam_q1[j].reshape(1, DIFF_HEAD_DIM), lam_k1[j].reshape(1, DIFF_HEAD_DIM),
            lam_q2[j].reshape(1, DIFF_HEAD_DIM), lam_k2[j].reshape(1, DIFF_HEAD_DIM))
    return x2.reshape(batch, seq, D_MODEL)
```

```python
import functools
import math

import jax
import jax.numpy as jnp
from jax import lax
from jax.experimental import pallas as pl
from jax.experimental.pallas import tpu as pltpu

F32 = jnp.float32
BF16 = jnp.bfloat16

D_MODEL = 1024
DEPTH = 4
N_A_LAYERS = DEPTH // 2
CHUNK = 64
EPS = 1e-6
ROPE_THETA = 10000.0

RET_HEADS = 8
RET_QK_DIM = 128
RET_V_DIM = 256
RET_QK_WIDTH = RET_HEADS * RET_QK_DIM
RET_WIDTH = RET_HEADS * RET_V_DIM
RET_IN_WIDTH = 2 * RET_QK_WIDTH + 2 * RET_WIDTH

DIFF_HEADS = 8
DIFF_HEAD_DIM = 64
DIFF_V_DIM = 128
DIFF_QK_WIDTH = DIFF_HEADS * 2 * DIFF_HEAD_DIM
DIFF_WIDTH = DIFF_HEADS * DIFF_V_DIM

LANES = 128
VMEM_LIMIT = 56 * 1024 * 1024

PROJ_TM = 512
RET_BLOCK = 1024
RET_CHUNK = 256
ATT_TQ = 512
ATT_TK = 512
ATT_ONES_ROWS = 16
MASK_NEG = -(2.0 ** 100)

NT_DIMS = (((1,), (1,)), ((), ()))
TN_DIMS = (((0,), (0,)), ((), ()))


def _params(sem):
    return pltpu.CompilerParams(dimension_semantics=sem, vmem_limit_bytes=VMEM_LIMIT)


def _normed(x_ref, g_ref):
    x = x_ref[...]
    ms = jnp.mean(x * x, axis=-1, keepdims=True)
    return (x * lax.rsqrt(ms + EPS) * g_ref[...]).astype(BF16)


def _silu(g):
    return g / (1.0 + jnp.exp(-g))


def _ret_in_proj_kernel(x_ref, g_ref, w_ref, cos_ref, sin_ref, o_ref):
    h = _normed(x_ref, g_ref)
    cos = cos_ref[...]
    sin = sin_ref[...]
    k_scale = RET_QK_DIM ** -0.5
    for j in range(RET_IN_WIDTH // D_MODEL):
        cols = slice(j * D_MODEL, (j + 1) * D_MODEL)
        r = jnp.dot(h, w_ref[:, cols], preferred_element_type=F32)
        if j < 2:
            for s in range(RET_HEADS):
                xs = r[:, s * LANES:(s + 1) * LANES]
                y = xs * cos + pltpu.roll(xs, RET_QK_DIM // 2, 1) * sin
                if j == 1:
                    y = y * k_scale
                o_ref[:, j * D_MODEL + s * LANES:j * D_MODEL + (s + 1) * LANES] = y.astype(BF16)
        else:
            o_ref[:, cols] = r.astype(BF16)


def _retention_kernel(q_ref, k_ref, v_ref, gate_ref, dm_ref, qd_ref, kd_ref, o_ref, state_ref):
    @pl.when(pl.program_id(2) == 0)
    def _():
        state_ref[...] = jnp.zeros_like(state_ref)

    dm = dm_ref[0]
    qd = qd_ref[0]
    kd = kd_ref[0]
    cdec = qd[RET_CHUNK - 1:RET_CHUNK, :]
    for i in range(RET_BLOCK // RET_CHUNK):
        rows = slice(i * RET_CHUNK, (i + 1) * RET_CHUNK)
        q = q_ref[rows, :]
        k = k_ref[rows, :]
        v = v_ref[rows, :]
        s = lax.dot_general(q, k, NT_DIMS, preferred_element_type=F32) * dm
        st = state_ref[...]
        o = jnp.dot(s.astype(BF16), v, preferred_element_type=F32)
        o = o + qd * jnp.dot(q, st.astype(BF16), preferred_element_type=F32)
        kdk = (k.astype(F32) * kd).astype(BF16)
        state_ref[...] = st * cdec + lax.dot_general(kdk, v, TN_DIMS, preferred_element_type=F32)
        mu = jnp.mean(o, axis=-1, keepdims=True)
        d = o - mu
        var = jnp.mean(d * d, axis=-1, keepdims=True)
        on = d * lax.rsqrt(var + EPS)
        o_ref[rows, :] = (on * _silu(gate_ref[rows, :].astype(F32))).astype(BF16)


def _out_proj_kernel(o_ref, w_ref, x_ref, g_ref, xo_ref):
    y = jnp.dot(o_ref[...], w_ref[...], preferred_element_type=F32)
    ms = jnp.mean(y * y, axis=-1, keepdims=True)
    xo_ref[...] = x_ref[...] + y * lax.rsqrt(ms + EPS) * g_ref[...]


def _retention_tables():
    h = jnp.arange(RET_HEADS, dtype=F32)
    log_gamma = jnp.log1p(-jnp.exp2(-5.0 - h))
    pos = jnp.arange(RET_CHUNK, dtype=F32)
    diff = pos[:, None] - pos[None, :]
    dm = jnp.where(diff[None] >= 0,
                   jnp.exp(jnp.maximum(diff, 0.0)[None] * log_gamma[:, None, None]), 0.0)
    qd = jnp.exp((pos[None, :] + 1.0) * log_gamma[:, None])
    kd = jnp.exp((RET_CHUNK - 1.0 - pos[None, :]) * log_gamma[:, None])
    qd = jnp.broadcast_to(qd[:, :, None], (RET_HEADS, RET_CHUNK, RET_V_DIM))
    kd = jnp.broadcast_to(kd[:, :, None], (RET_HEADS, RET_CHUNK, RET_QK_DIM))
    return dm, qd, kd


def _rope_tables_rows(seq, dim):
    half = dim // 2
    inv = ROPE_THETA ** (-jnp.arange(half, dtype=F32) / half)
    ang = jnp.arange(seq, dtype=F32)[:, None] * inv[None, :]
    cos = jnp.cos(ang)
    sin = jnp.sin(ang)
    reps = LANES // dim
    cos_full = jnp.tile(jnp.concatenate([cos, cos], axis=1), (1, reps))
    sin_signed = jnp.tile(jnp.concatenate([-sin, sin], axis=1), (1, reps))
    return cos_full, sin_signed


def _retention_layer(x2, batch, seq, pre_g, post_g, w_in, w_out, rope_tabs, ret_tabs):
    tokens = x2.shape[0]
    cos, sin = rope_tabs
    n_seq_tiles = seq // PROJ_TM
    proj = pl.pallas_call(
        _ret_in_proj_kernel,
        out_shape=jax.ShapeDtypeStruct((tokens, RET_IN_WIDTH), BF16),
        grid=(tokens // PROJ_TM,),
        in_specs=[
            pl.BlockSpec((PROJ_TM, D_MODEL), lambda i: (i, 0)),
            pl.BlockSpec((1, D_MODEL), lambda i: (0, 0)),
            pl.BlockSpec((D_MODEL, RET_IN_WIDTH), lambda i: (0, 0)),
            pl.BlockSpec((PROJ_TM, LANES), lambda i: (i % n_seq_tiles, 0)),
            pl.BlockSpec((PROJ_TM, LANES), lambda i: (i % n_seq_tiles, 0)),
        ],
        out_specs=pl.BlockSpec((PROJ_TM, RET_IN_WIDTH), lambda i: (i, 0)),
        compiler_params=_params(("arbitrary",)),
        name="ret_in_proj",
    )(x2, pre_g, w_in, cos, sin)

    dm, qd, kd = ret_tabs
    nblk = seq // RET_BLOCK
    v_off = 2 * RET_QK_WIDTH // RET_V_DIM
    g_off = (2 * RET_QK_WIDTH + RET_WIDTH) // RET_V_DIM
    o = pl.pallas_call(
        _retention_kernel,
        out_shape=jax.ShapeDtypeStruct((tokens, RET_WIDTH), BF16),
        grid=(batch, RET_HEADS, nblk),
        in_specs=[
            pl.BlockSpec((RET_BLOCK, RET_QK_DIM), lambda b, h, c: (b * nblk + c, h)),
            pl.BlockSpec((RET_BLOCK, RET_QK_DIM), lambda b, h, c: (b * nblk + c, RET_HEADS + h)),
            pl.BlockSpec((RET_BLOCK, RET_V_DIM), lambda b, h, c: (b * nblk + c, v_off + h)),
            pl.BlockSpec((RET_BLOCK, RET_V_DIM), lambda b, h, c: (b * nblk + c, g_off + h)),
            pl.BlockSpec((1, RET_CHUNK, RET_CHUNK), lambda b, h, c: (h, 0, 0)),
            pl.BlockSpec((1, RET_CHUNK, RET_V_DIM), lambda b, h, c: (h, 0, 0)),
            pl.BlockSpec((1, RET_CHUNK, RET_QK_DIM), lambda b, h, c: (h, 0, 0)),
        ],
        out_specs=pl.BlockSpec((RET_BLOCK, RET_V_DIM), lambda b, h, c: (b * nblk + c, h)),
        scratch_shapes=[pltpu.VMEM((RET_QK_DIM, RET_V_DIM), F32)],
        compiler_params=_params(("arbitrary", "arbitrary", "arbitrary")),
        name="retention",
    )(proj, proj, proj, proj, dm, qd, kd)

    return pl.pallas_call(
        _out_proj_kernel,
        out_shape=jax.ShapeDtypeStruct((tokens, D_MODEL), F32),
        grid=(tokens // PROJ_TM,),
        in_specs=[
            pl.BlockSpec((PROJ_TM, RET_WIDTH), lambda i: (i, 0)),
            pl.BlockSpec((RET_WIDTH, D_MODEL), lambda i: (0, 0)),
            pl.BlockSpec((PROJ_TM, D_MODEL), lambda i: (i, 0)),
            pl.BlockSpec((1, D_MODEL), lambda i: (0, 0)),
        ],
        out_specs=pl.BlockSpec((PROJ_TM, D_MODEL), lambda i: (i, 0)),
        compiler_params=_params(("arbitrary",)),
        name="ret_out_proj",
    )(o, w_out, x2, post_g)


def _kv_proj_kernel(x_ref, g_ref, wk_ref, wvt_ref, cos_ref, sin_ref, k_ref, vt_ref):
    h = _normed(x_ref, g_ref)
    cos = cos_ref[...]
    sin = sin_ref[...]
    kf = jnp.dot(h, wk_ref[...], preferred_element_type=F32)
    lane = lax.broadcasted_iota(jnp.int32, (PROJ_TM, LANES), 1)
    first_half = (lane & (DIFF_HEAD_DIM - 1)) < (DIFF_HEAD_DIM // 2)
    for s in range(DIFF_QK_WIDTH // LANES):
        xs = kf[:, s * LANES:(s + 1) * LANES]
        rot = jnp.where(first_half,
                        pltpu.roll(xs, LANES - DIFF_HEAD_DIM // 2, 1),
                        pltpu.roll(xs, DIFF_HEAD_DIM // 2, 1))
        k_ref[:, s * LANES:(s + 1) * LANES] = (xs * cos + rot * sin).astype(BF16)
    vt = lax.dot_general(wvt_ref[...], h, NT_DIMS, preferred_element_type=F32)
    vt_ref[0] = vt.astype(BF16)


def _q_proj_kernel(x_ref, g_ref, wt_ref, cos_ref, sin_ref, qt_ref, gt_ref):
    h = _normed(x_ref, g_ref)
    pt = lax.dot_general(wt_ref[...], h, NT_DIMS, preferred_element_type=F32)
    cos = cos_ref[...]
    sin = sin_ref[...]
    scale = DIFF_HEAD_DIM ** -0.5 * math.log2(math.e)
    half = DIFF_HEAD_DIM // 2
    for grp in range(DIFF_QK_WIDTH // DIFF_HEAD_DIM):
        r0 = grp * DIFF_HEAD_DIM
        x1 = pt[r0:r0 + half, :]
        x2 = pt[r0 + half:r0 + DIFF_HEAD_DIM, :]
        qt_ref[0, r0:r0 + half, :] = ((x1 * cos - x2 * sin) * scale).astype(BF16)
        qt_ref[0, r0 + half:r0 + DIFF_HEAD_DIM, :] = ((x1 * sin + x2 * cos) * scale).astype(BF16)
    gt_ref[0] = pt[DIFF_QK_WIDTH:, :].astype(BF16)


def _diff_attn_kernel(qt_ref, k_ref, vt_ref, gt_ref, sub_ref, lq1_ref, lk1_ref, lq2_ref, lk2_ref,
                      ot_ref, qbd_ref, acc_ref, m_ref, sa_ref, sb_ref, sc_ref, *, lambda_init):
    qi = pl.program_id(2)
    tq, tk = ATT_TQ, ATT_TK
    dh = DIFF_HEAD_DIM
    qt = qt_ref[0]
    zeros = jnp.zeros((dh, tq), BF16)
    qbd_ref[0:dh, 0:tq] = qt[0:dh, :]
    qbd_ref[0:dh, tq:2 * tq] = zeros
    qbd_ref[dh:2 * dh, 0:tq] = zeros
    qbd_ref[dh:2 * dh, tq:2 * tq] = qt[dh:2 * dh, :]
    m_ref[...] = jnp.full(m_ref.shape, MASK_NEG, F32)
    acc_ref[...] = jnp.zeros(acc_ref.shape, F32)

    def scores(s_ref, j):
        kk = pl.multiple_of(j * tk, tk)
        s = jnp.dot(k_ref[0, pl.ds(kk, tk), :], qbd_ref[...], preferred_element_type=F32)
        s_ref[...] = s.astype(BF16)

    ones_rows = jnp.ones((ATT_ONES_ROWS, tk), BF16)

    def consume(s_ref, j, diagonal):
        kk = pl.multiple_of(j * tk, tk)
        m_old = m_ref[...]
        if diagonal:
            s = s_ref[...].astype(F32)
            key_chunk = lax.broadcasted_iota(jnp.int32, (tk, 2 * tq), 0) // CHUNK
            qry_chunk = (lax.broadcasted_iota(jnp.int32, (tk, 2 * tq), 1) % tq) // CHUNK
            s = jnp.where(key_chunk <= qry_chunk, s, MASK_NEG)
            m_new = jnp.maximum(m_old, jnp.max(s, axis=0, keepdims=True))
            p = jnp.exp2(s - m_new).astype(BF16)
        else:
            s = s_ref[...]
            m_new = jnp.maximum(m_old, jnp.max(s, axis=0, keepdims=True).astype(F32))
            p = jnp.exp2(s - m_new.astype(BF16))
        alpha = jnp.exp2(m_old - m_new)
        vt_aug = jnp.concatenate([vt_ref[0, :, pl.ds(kk, tk)], ones_rows], axis=0)
        pv = jnp.dot(vt_aug, p, preferred_element_type=F32)
        acc_ref[...] = alpha * acc_ref[...] + pv
        m_ref[...] = m_new

    scores(sa_ref, 0)

    @pl.when(qi >= 1)
    def _():
        scores(sb_ref, 1)

    def steady(i, carry):
        j = 3 * i
        scores(sc_ref, j + 2)
        consume(sa_ref, j, False)
        scores(sa_ref, j + 3)
        consume(sb_ref, j + 1, False)
        scores(sb_ref, j + 4)
        consume(sc_ref, j + 2, False)
        return carry

    trips = jnp.maximum(qi - 1, 0) // 3
    lax.fori_loop(0, trips, steady, 0)
    j0 = 3 * trips
    left = qi + 1 - j0

    @pl.when(left == 1)
    def _():
        consume(sa_ref, qi, True)

    @pl.when(left == 2)
    def _():
        consume(sa_ref, j0, False)
        consume(sb_ref, qi, True)

    @pl.when(left == 3)
    def _():
        scores(sc_ref, qi)
        consume(sa_ref, j0, False)
        consume(sb_ref, j0 + 1, False)
        consume(sc_ref, qi, True)

    @pl.when(left == 4)
    def _():
        scores(sc_ref, j0 + 2)
        consume(sa_ref, j0, False)
        scores(sa_ref, qi)
        consume(sb_ref, j0 + 1, False)
        consume(sc_ref, j0 + 2, False)
        consume(sa_ref, qi, True)

    dv = DIFF_V_DIM
    a = acc_ref[0:dv, :] * (1.0 / acc_ref[dv:dv + 1, :])
    lam = (jnp.exp(jnp.sum(lq1_ref[...] * lk1_ref[...], axis=-1, keepdims=True))
           - jnp.exp(jnp.sum(lq2_ref[...] * lk2_ref[...], axis=-1, keepdims=True))
           + lambda_init)
    o = a[:, 0:tq] - lam * a[:, tq:2 * tq]
    ms = jnp.mean(o * o, axis=0, keepdims=True)
    on = o * lax.rsqrt(ms + EPS) * sub_ref[...] * (1.0 - lambda_init)
    ot_ref[0] = (on * _silu(gt_ref[0].astype(F32))).astype(BF16)


def _out_proj_t_kernel(ot_ref, w_ref, x_ref, g_ref, xo_ref):
    y = lax.dot_general(ot_ref[0], w_ref[...], TN_DIMS, preferred_element_type=F32)
    ms = jnp.mean(y * y, axis=-1, keepdims=True)
    xo_ref[...] = x_ref[...] + y * lax.rsqrt(ms + EPS) * g_ref[...]


def _kv_proj(x2, batch, seq, kv_g, wk, wvt, rope_tabs):
    tokens = x2.shape[0]
    cos, sin = rope_tabs
    nst = seq // PROJ_TM
    return pl.pallas_call(
        _kv_proj_kernel,
        out_shape=(jax.ShapeDtypeStruct((tokens, DIFF_QK_WIDTH), BF16),
                   jax.ShapeDtypeStruct((batch, DIFF_WIDTH, seq), BF16)),
        grid=(batch, nst),
        in_specs=[
            pl.BlockSpec((PROJ_TM, D_MODEL), lambda b, i: (b * nst + i, 0)),
            pl.BlockSpec((1, D_MODEL), lambda b, i: (0, 0)),
            pl.BlockSpec((D_MODEL, DIFF_QK_WIDTH), lambda b, i: (0, 0)),
            pl.BlockSpec((DIFF_WIDTH, D_MODEL), lambda b, i: (0, 0)),
            pl.BlockSpec((PROJ_TM, LANES), lambda b, i: (i, 0)),
            pl.BlockSpec((PROJ_TM, LANES), lambda b, i: (i, 0)),
        ],
        out_specs=(pl.BlockSpec((PROJ_TM, DIFF_QK_WIDTH), lambda b, i: (b * nst + i, 0)),
                   pl.BlockSpec((1, DIFF_WIDTH, PROJ_TM), lambda b, i: (b, 0, i))),
        compiler_params=_params(("arbitrary", "arbitrary")),
        name="kv_proj",
    )(x2, kv_g, wk, wvt, cos, sin)


def _diff_layer(x2, batch, seq, layer, pre_g, post_g, wt_in, w_out, k_sh, vt_sh, rope_t_tabs,
                sub_g, lq1, lk1, lq2, lk2):
    tokens = x2.shape[0]
    cos_t, sin_t = rope_t_tabs
    nst = seq // PROJ_TM
    qt, gt = pl.pallas_call(
        _q_proj_kernel,
        out_shape=(jax.ShapeDtypeStruct((batch, DIFF_QK_WIDTH, seq), BF16),
                   jax.ShapeDtypeStruct((batch, DIFF_WIDTH, seq), BF16)),
        grid=(batch, nst),
        in_specs=[
            pl.BlockSpec((PROJ_TM, D_MODEL), lambda b, i: (b * nst + i, 0)),
            pl.BlockSpec((1, D_MODEL), lambda b, i: (0, 0)),
            pl.BlockSpec((DIFF_QK_WIDTH + DIFF_WIDTH, D_MODEL), lambda b, i: (0, 0)),
            pl.BlockSpec((DIFF_HEAD_DIM // 2, PROJ_TM), lambda b, i: (0, i)),
            pl.BlockSpec((DIFF_HEAD_DIM // 2, PROJ_TM), lambda b, i: (0, i)),
        ],
        out_specs=(pl.BlockSpec((1, DIFF_QK_WIDTH, PROJ_TM), lambda b, i: (b, 0, i)),
                   pl.BlockSpec((1, DIFF_WIDTH, PROJ_TM), lambda b, i: (b, 0, i))),
        compiler_params=_params(("arbitrary", "arbitrary")),
        name="q_proj",
    )(x2, pre_g, wt_in, cos_t, sin_t)

    lambda_init = 0.8 - 0.6 * math.exp(-0.3 * layer)
    nq = seq // ATT_TQ
    k3 = k_sh.reshape(batch, seq, DIFF_QK_WIDTH)
    lam_spec = pl.BlockSpec((1, DIFF_HEAD_DIM), lambda b, h, q: (0, 0))
    ot = pl.pallas_call(
        functools.partial(_diff_attn_kernel, lambda_init=lambda_init),
        out_shape=jax.ShapeDtypeStruct((batch, DIFF_WIDTH, seq), BF16),
        grid=(batch, DIFF_HEADS, nq),
        in_specs=[
            pl.BlockSpec((1, 2 * DIFF_HEAD_DIM, ATT_TQ), lambda b, h, q: (b, h, q)),
            pl.BlockSpec((1, seq, 2 * DIFF_HEAD_DIM), lambda b, h, q: (b, 0, h)),
            pl.BlockSpec((1, DIFF_V_DIM, seq), lambda b, h, q: (b, h, 0)),
            pl.BlockSpec((1, DIFF_V_DIM, ATT_TQ), lambda b, h, q: (b, h, q)),
            pl.BlockSpec((DIFF_V_DIM, 1), lambda b, h, q: (0, 0)),
            lam_spec, lam_spec, lam_spec, lam_spec,
        ],
        out_specs=pl.BlockSpec((1, DIFF_V_DIM, ATT_TQ), lambda b, h, q: (b, h, q)),
        scratch_shapes=[
            pltpu.VMEM((2 * DIFF_HEAD_DIM, 2 * ATT_TQ), BF16),
            pltpu.VMEM((DIFF_V_DIM + ATT_ONES_ROWS, 2 * ATT_TQ), F32),
            pltpu.VMEM((1, 2 * ATT_TQ), F32),
            pltpu.VMEM((ATT_TK, 2 * ATT_TQ), BF16),
            pltpu.VMEM((ATT_TK, 2 * ATT_TQ), BF16),
            pltpu.VMEM((ATT_TK, 2 * ATT_TQ), BF16),
        ],
        compiler_params=_params(("arbitrary", "arbitrary", "arbitrary")),
        name="diff_attn",
    )(qt, k3, vt_sh, gt, sub_g, lq1, lk1, lq2, lk2)

    return pl.pallas_call(
        _out_proj_t_kernel,
        out_shape=jax.ShapeDtypeStruct((tokens, D_MODEL), F32),
        grid=(batch, nst),
        in_specs=[
            pl.BlockSpec((1, DIFF_WIDTH, PROJ_TM), lambda b, i: (b, 0, i)),
            pl.BlockSpec((DIFF_WIDTH, D_MODEL), lambda b, i: (0, 0)),
            pl.BlockSpec((PROJ_TM, D_MODEL), lambda b, i: (b * nst + i, 0)),
            pl.BlockSpec((1, D_MODEL), lambda b, i: (0, 0)),
        ],
        out_specs=pl.BlockSpec((PROJ_TM, D_MODEL), lambda b, i: (b * nst + i, 0)),
        compiler_params=_params(("arbitrary", "arbitrary")),
        name="diff_out_proj",
    )(ot, w_out, x2, post_g)


def kernel(x, pre_norm, post_norm, w_in_a, w_out_a, kv_norm, w_kv, w_in_b,
           lam_q1, lam_k1, lam_q2, lam_k2, sub_norm_b, w_out_b):
    batch, seq, _ = x.shape
    x2 = x.reshape(batch * seq, D_MODEL)

    ret_rope = _rope_tables_rows(seq, RET_QK_DIM)
    ret_tabs = _retention_tables()
    for layer in range(N_A_LAYERS):
        x2 = _retention_layer(
            x2, batch, seq,
            pre_norm[layer].reshape(1, D_MODEL), post_norm[layer].reshape(1, D_MODEL),
            w_in_a[layer].astype(BF16), w_out_a[layer].astype(BF16), ret_rope, ret_tabs)

    diff_rope = _rope_tables_rows(seq, DIFF_HEAD_DIM)
    k_sh, vt_sh = _kv_proj(
        x2, batch, seq, kv_norm.reshape(1, D_MODEL),
        w_kv[:, :DIFF_QK_WIDTH].astype(BF16), w_kv[:, DIFF_QK_WIDTH:].T.astype(BF16), diff_rope)

    half = DIFF_HEAD_DIM // 2
    inv = ROPE_THETA ** (-jnp.arange(half, dtype=F32) / half)
    ang_t = inv[:, None] * jnp.arange(seq, dtype=F32)[None, :]
    rope_t = (jnp.cos(ang_t), jnp.sin(ang_t))
    for layer in range(N_A_LAYERS, DEPTH):
        j = layer - N_A_LAYERS
        x2 = _diff_layer(
            x2, batch, seq, layer,
            pre_norm[layer].reshape(1, D_MODEL), post_norm[layer].reshape(1, D_MODEL),
            w_in_b[j].T.astype(BF16), w_out_b[j].astype(BF16), k_sh, vt_sh, rope_t,
            sub_norm_b[j].reshape(DIFF_V_DIM, 1),
            lam_q1[j].reshape(1, DIFF_HEAD_DIM), lam_k1[j].reshape(1, DIFF_HEAD_DIM),
            lam_q2[j].reshape(1, DIFF_HEAD_DIM), lam_k2[j].reshape(1, DIFF_HEAD_DIM))
    return x2.reshape(batch, seq, D_MODEL)
```

```python
import functools
import math

import jax
import jax.numpy as jnp
from jax import lax
from jax.experimental import pallas as pl
from jax.experimental.pallas import tpu as pltpu

F32 = jnp.float32
BF16 = jnp.bfloat16

D_MODEL = 1024
DEPTH = 4
N_A_LAYERS = DEPTH // 2
CHUNK = 64
EPS = 1e-6
ROPE_THETA = 10000.0

RET_HEADS = 8
RET_QK_DIM = 128
RET_V_DIM = 256
RET_QK_WIDTH = RET_HEADS * RET_QK_DIM
RET_WIDTH = RET_HEADS * RET_V_DIM
RET_IN_WIDTH = 2 * RET_QK_WIDTH + 2 * RET_WIDTH

DIFF_HEADS = 8
DIFF_HEAD_DIM = 64
DIFF_V_DIM = 128
DIFF_QK_WIDTH = DIFF_HEADS * 2 * DIFF_HEAD_DIM
DIFF_WIDTH = DIFF_HEADS * DIFF_V_DIM

LANES = 128
VMEM_LIMIT = 56 * 1024 * 1024

PROJ_TM = 512
PROJ_ROWS = 512
OUT_ROWS = 256
RET_BLOCK = 1024
RET_CHUNK = 256
ATT_TQ = 512
ATT_TK = 512
ATT_SLAB = 256
ATT_ONES_ROWS = 16
ATT_VT_ROWS = DIFF_V_DIM + ATT_ONES_ROWS
MASK_NEG = -(2.0 ** 100)

NT_DIMS = (((1,), (1,)), ((), ()))
TN_DIMS = (((0,), (0,)), ((), ()))


def _params(sem):
    return pltpu.CompilerParams(dimension_semantics=sem, vmem_limit_bytes=VMEM_LIMIT)


def _normed(x_ref, g_ref):
    x = x_ref[...]
    ms = jnp.mean(x * x, axis=-1, keepdims=True)
    return (x * lax.rsqrt(ms + EPS) * g_ref[...]).astype(BF16)


def _silu(g):
    return g / (1.0 + jnp.exp(-g))


def _ret_in_proj_kernel(x_ref, g_ref, w_ref, cos_ref, sin_ref, o_ref):
    h = _normed(x_ref, g_ref)
    cos = cos_ref[...]
    sin = sin_ref[...]
    k_scale = RET_QK_DIM ** -0.5
    for j in range(RET_IN_WIDTH // D_MODEL):
        cols = slice(j * D_MODEL, (j + 1) * D_MODEL)
        r = jnp.dot(h, w_ref[:, cols], preferred_element_type=F32)
        if j < 2:
            for s in range(RET_HEADS):
                xs = r[:, s * LANES:(s + 1) * LANES]
                y = xs * cos + pltpu.roll(xs, RET_QK_DIM // 2, 1) * sin
                if j == 1:
                    y = y * k_scale
                o_ref[:, j * D_MODEL + s * LANES:j * D_MODEL + (s + 1) * LANES] = y.astype(BF16)
        else:
            o_ref[:, cols] = r.astype(BF16)


def _retention_kernel(q_ref, k_ref, v_ref, gate_ref, dm_ref, qd_ref, kd_ref, o_ref, state_ref):
    @pl.when(pl.program_id(2) == 0)
    def _():
        state_ref[...] = jnp.zeros_like(state_ref)

    dm = dm_ref[0]
    qd = qd_ref[0]
    kd = kd_ref[0]
    cdec = qd[RET_CHUNK - 1:RET_CHUNK, :]
    for i in range(RET_BLOCK // RET_CHUNK):
        rows = slice(i * RET_CHUNK, (i + 1) * RET_CHUNK)
        q = q_ref[rows, :]
        k = k_ref[rows, :]
        v = v_ref[rows, :]
        s = lax.dot_general(q, k, NT_DIMS, preferred_element_type=F32) * dm
        st = state_ref[...]
        o = jnp.dot(s.astype(BF16), v, preferred_element_type=F32)
        o = o + qd * jnp.dot(q, st.astype(BF16), preferred_element_type=F32)
        kdk = (k.astype(F32) * kd).astype(BF16)
        state_ref[...] = st * cdec + lax.dot_general(kdk, v, TN_DIMS, preferred_element_type=F32)
        mu = jnp.mean(o, axis=-1, keepdims=True)
        d = o - mu
        var = jnp.mean(d * d, axis=-1, keepdims=True)
        on = d * lax.rsqrt(var + EPS)
        o_ref[rows, :] = (on * _silu(gate_ref[rows, :].astype(F32))).astype(BF16)


def _out_proj_kernel(o_ref, w_ref, x_ref, g_ref, xo_ref):
    g = g_ref[...]
    for c in range(PROJ_TM // OUT_ROWS):
        rows = slice(c * OUT_ROWS, (c + 1) * OUT_ROWS)
        y = jnp.dot(o_ref[rows, :], w_ref[...], preferred_element_type=F32)
        ms = jnp.mean(y * y, axis=-1, keepdims=True)
        xo_ref[rows, :] = x_ref[rows, :] + y * lax.rsqrt(ms + EPS) * g


def _retention_tables():
    h = jnp.arange(RET_HEADS, dtype=F32)
    log_gamma = jnp.log1p(-jnp.exp2(-5.0 - h))
    pos = jnp.arange(RET_CHUNK, dtype=F32)
    diff = pos[:, None] - pos[None, :]
    dm = jnp.where(diff[None] >= 0,
                   jnp.exp(jnp.maximum(diff, 0.0)[None] * log_gamma[:, None, None]), 0.0)
    qd = jnp.exp((pos[None, :] + 1.0) * log_gamma[:, None])
    kd = jnp.exp((RET_CHUNK - 1.0 - pos[None, :]) * log_gamma[:, None])
    qd = jnp.broadcast_to(qd[:, :, None], (RET_HEADS, RET_CHUNK, RET_V_DIM))
    kd = jnp.broadcast_to(kd[:, :, None], (RET_HEADS, RET_CHUNK, RET_QK_DIM))
    return dm, qd, kd


def _rope_tables_rows(seq, dim):
    half = dim // 2
    inv = ROPE_THETA ** (-jnp.arange(half, dtype=F32) / half)
    ang = jnp.arange(seq, dtype=F32)[:, None] * inv[None, :]
    cos = jnp.cos(ang)
    sin = jnp.sin(ang)
    reps = LANES // dim
    cos_full = jnp.tile(jnp.concatenate([cos, cos], axis=1), (1, reps))
    sin_signed = jnp.tile(jnp.concatenate([-sin, sin], axis=1), (1, reps))
    return cos_full, sin_signed


def _retention_layer(x2, batch, seq, pre_g, post_g, w_in, w_out, rope_tabs, ret_tabs):
    tokens = x2.shape[0]
    cos, sin = rope_tabs
    n_seq_tiles = seq // PROJ_TM
    proj = pl.pallas_call(
        _ret_in_proj_kernel,
        out_shape=jax.ShapeDtypeStruct((tokens, RET_IN_WIDTH), BF16),
        grid=(tokens // PROJ_TM,),
        in_specs=[
            pl.BlockSpec((PROJ_TM, D_MODEL), lambda i: (i, 0)),
            pl.BlockSpec((1, D_MODEL), lambda i: (0, 0)),
            pl.BlockSpec((D_MODEL, RET_IN_WIDTH), lambda i: (0, 0)),
            pl.BlockSpec((PROJ_TM, LANES), lambda i: (i % n_seq_tiles, 0)),
            pl.BlockSpec((PROJ_TM, LANES), lambda i: (i % n_seq_tiles, 0)),
        ],
        out_specs=pl.BlockSpec((PROJ_TM, RET_IN_WIDTH), lambda i: (i, 0)),
        compiler_params=_params(("arbitrary",)),
        name="ret_in_proj",
    )(x2, pre_g, w_in, cos, sin)

    dm, qd, kd = ret_tabs
    nblk = seq // RET_BLOCK
    v_off = 2 * RET_QK_WIDTH // RET_V_DIM
    g_off = (2 * RET_QK_WIDTH + RET_WIDTH) // RET_V_DIM
    o = pl.pallas_call(
        _retention_kernel,
        out_shape=jax.ShapeDtypeStruct((tokens, RET_WIDTH), BF16),
        grid=(batch, RET_HEADS, nblk),
        in_specs=[
            pl.BlockSpec((RET_BLOCK, RET_QK_DIM), lambda b, h, c: (b * nblk + c, h)),
            pl.BlockSpec((RET_BLOCK, RET_QK_DIM), lambda b, h, c: (b * nblk + c, RET_HEADS + h)),
            pl.BlockSpec((RET_BLOCK, RET_V_DIM), lambda b, h, c: (b * nblk + c, v_off + h)),
            pl.BlockSpec((RET_BLOCK, RET_V_DIM), lambda b, h, c: (b * nblk + c, g_off + h)),
            pl.BlockSpec((1, RET_CHUNK, RET_CHUNK), lambda b, h, c: (h, 0, 0)),
            pl.BlockSpec((1, RET_CHUNK, RET_V_DIM), lambda b, h, c: (h, 0, 0)),
            pl.BlockSpec((1, RET_CHUNK, RET_QK_DIM), lambda b, h, c: (h, 0, 0)),
        ],
        out_specs=pl.BlockSpec((RET_BLOCK, RET_V_DIM), lambda b, h, c: (b * nblk + c, h)),
        scratch_shapes=[pltpu.VMEM((RET_QK_DIM, RET_V_DIM), F32)],
        compiler_params=_params(("arbitrary", "arbitrary", "arbitrary")),
        name="retention",
    )(proj, proj, proj, proj, dm, qd, kd)

    return pl.pallas_call(
        _out_proj_kernel,
        out_shape=jax.ShapeDtypeStruct((tokens, D_MODEL), F32),
        grid=(tokens // PROJ_TM,),
        in_specs=[
            pl.BlockSpec((PROJ_TM, RET_WIDTH), lambda i: (i, 0)),
            pl.BlockSpec((RET_WIDTH, D_MODEL), lambda i: (0, 0)),
            pl.BlockSpec((PROJ_TM, D_MODEL), lambda i: (i, 0)),
            pl.BlockSpec((1, D_MODEL), lambda i: (0, 0)),
        ],
        out_specs=pl.BlockSpec((PROJ_TM, D_MODEL), lambda i: (i, 0)),
        compiler_params=_params(("arbitrary",)),
        name="ret_out_proj",
    )(o, w_out, x2, post_g)


def _kv_proj_kernel(x_ref, g_ref, wk_ref, wvt_ref, cos_ref, sin_ref, k_ref, vt_ref):
    h = _normed(x_ref, g_ref)
    cos = cos_ref[...]
    sin = sin_ref[...]
    kf = jnp.dot(h, wk_ref[...], preferred_element_type=F32)
    lane = lax.broadcasted_iota(jnp.int32, (PROJ_TM, LANES), 1)
    first_half = (lane & (DIFF_HEAD_DIM - 1)) < (DIFF_HEAD_DIM // 2)
    for s in range(DIFF_QK_WIDTH // LANES):
        xs = kf[:, s * LANES:(s + 1) * LANES]
        rot = jnp.where(first_half,
                        pltpu.roll(xs, LANES - DIFF_HEAD_DIM // 2, 1),
                        pltpu.roll(xs, DIFF_HEAD_DIM // 2, 1))
        k_ref[:, s * LANES:(s + 1) * LANES] = (xs * cos + rot * sin).astype(BF16)
    vt = lax.dot_general(wvt_ref[...], h, NT_DIMS, preferred_element_type=F32)
    ones_rows = jnp.ones((ATT_ONES_ROWS, PROJ_TM), BF16)
    for hd in range(DIFF_HEADS):
        r0 = hd * ATT_VT_ROWS
        vt_ref[0, r0:r0 + DIFF_V_DIM, :] = vt[hd * DIFF_V_DIM:(hd + 1) * DIFF_V_DIM, :].astype(BF16)
        vt_ref[0, r0 + DIFF_V_DIM:r0 + ATT_VT_ROWS, :] = ones_rows


def _q_proj_kernel(x_ref, g_ref, wt_ref, cos_ref, sin_ref, qt_ref, gt_ref):
    h = _normed(x_ref, g_ref)
    cos = cos_ref[...]
    sin = sin_ref[...]
    scale = DIFF_HEAD_DIM ** -0.5 * math.log2(math.e)
    half = DIFF_HEAD_DIM // 2
    for c in range((DIFF_QK_WIDTH + DIFF_WIDTH) // PROJ_ROWS):
        c0 = c * PROJ_ROWS
        pt = lax.dot_general(wt_ref[c0:c0 + PROJ_ROWS, :], h, NT_DIMS,
                             preferred_element_type=F32)
        if c0 < DIFF_QK_WIDTH:
            for grp in range(PROJ_ROWS // DIFF_HEAD_DIM):
                r0 = grp * DIFF_HEAD_DIM
                x1 = pt[r0:r0 + half, :]
                x2 = pt[r0 + half:r0 + DIFF_HEAD_DIM, :]
                qt_ref[0, c0 + r0:c0 + r0 + half, :] = ((x1 * cos - x2 * sin) * scale).astype(BF16)
                qt_ref[0, c0 + r0 + half:c0 + r0 + DIFF_HEAD_DIM, :] = (
                    (x1 * sin + x2 * cos) * scale).astype(BF16)
        else:
            g0 = c0 - DIFF_QK_WIDTH
            gt_ref[0, g0:g0 + PROJ_ROWS, :] = pt.astype(BF16)


def _diff_attn_kernel(qt_ref, k_ref, vt_ref, gt_ref, sub_ref, lq1_ref, lk1_ref, lq2_ref, lk2_ref,
                      ot_ref, qbd_ref, acc_ref, m_ref, sa_ref, sb_ref, sc_ref, ma_ref, mb_ref, mc_ref,
                      *, lambda_init, n_q_tiles):
    tq, tk = ATT_TQ, ATT_TK
    dh = DIFF_HEAD_DIM
    dv = DIFF_V_DIM
    slabs = [slice(c * ATT_SLAB, (c + 1) * ATT_SLAB) for c in range(2 * tq // ATT_SLAB)]
    buf_a, buf_b, buf_c = (sa_ref, ma_ref), (sb_ref, mb_ref), (sc_ref, mc_ref)
    lam = (jnp.exp(jnp.sum(lq1_ref[...] * lk1_ref[...], axis=-1, keepdims=True))
           - jnp.exp(jnp.sum(lq2_ref[...] * lk2_ref[...], axis=-1, keepdims=True))
           + lambda_init)

    def load_queries(qi):
        q0 = pl.multiple_of(qi * tq, tq)
        qt = qt_ref[0, :, pl.ds(q0, tq)]
        zeros = jnp.zeros((dh, tq), BF16)
        qbd_ref[0:dh, 0:tq] = qt[0:dh, :]
        qbd_ref[0:dh, tq:2 * tq] = zeros
        qbd_ref[dh:2 * dh, 0:tq] = zeros
        qbd_ref[dh:2 * dh, tq:2 * tq] = qt[dh:2 * dh, :]

    def scores(buf, j):
        s_ref, tile_max_ref = buf
        kk = pl.multiple_of(j * tk, tk)
        kt = k_ref[0, pl.ds(kk, tk), :]
        for cols in slabs:
            s = jnp.dot(kt, qbd_ref[:, cols], preferred_element_type=F32).astype(BF16)
            s_ref[:, cols] = s
            tile_max_ref[:, cols] = jnp.max(s, axis=0, keepdims=True).astype(F32)

    def consume(buf, j, diagonal):
        s_ref, tile_max_ref = buf
        kk = pl.multiple_of(j * tk, tk)
        vt = vt_ref[0, :, pl.ds(kk, tk)]
        for cols in slabs:
            m_old = m_ref[:, cols]
            if diagonal:
                s = s_ref[:, cols].astype(F32)
                key_chunk = lax.broadcasted_iota(jnp.int32, s.shape, 0) // CHUNK
                qry = cols.start % tq + lax.broadcasted_iota(jnp.int32, s.shape, 1)
                s = jnp.where(key_chunk <= qry // CHUNK, s, MASK_NEG)
                m_new = jnp.maximum(m_old, jnp.max(s, axis=0, keepdims=True))
                p = jnp.exp2(s - m_new).astype(BF16)
            else:
                m_new = jnp.maximum(m_old, tile_max_ref[:, cols])
                p = jnp.exp2(s_ref[:, cols] - m_new.astype(BF16))
            alpha = jnp.exp2(m_old - m_new)
            pv = jnp.dot(vt, p, preferred_element_type=F32)
            acc_ref[:, cols] = alpha * acc_ref[:, cols] + pv
            m_ref[:, cols] = m_new

    def steady(i, carry):
        j = 3 * i
        scores(buf_b, j + 1)
        consume(buf_a, j, False)
        scores(buf_c, j + 2)
        consume(buf_b, j + 1, False)
        scores(buf_a, j + 3)
        consume(buf_c, j + 2, False)
        return carry

    def query_tile(qi, carry):
        m_ref[...] = jnp.full(m_ref.shape, MASK_NEG, F32)
        acc_ref[...] = jnp.zeros(acc_ref.shape, F32)
        trips = qi // 3
        lax.fori_loop(0, trips, steady, 0)
        j0 = 3 * trips
        left = qi + 1 - j0

        @pl.when(left == 1)
        def _():
            consume(buf_a, qi, True)

        @pl.when(left == 2)
        def _():
            scores(buf_b, qi)
            consume(buf_a, j0, False)
            consume(buf_b, qi, True)

        @pl.when(left == 3)
        def _():
            scores(buf_b, j0 + 1)
            consume(buf_a, j0, False)
            scores(buf_c, qi)
            consume(buf_b, j0 + 1, False)
            consume(buf_c, qi, True)

        q0 = pl.multiple_of(qi * tq, tq)
        a = acc_ref[0:dv, :] * (1.0 / acc_ref[dv:dv + 1, :])
        load_queries(jnp.minimum(qi + 1, n_q_tiles - 1))
        scores(buf_a, 0)
        o = a[:, 0:tq] - lam * a[:, tq:2 * tq]
        ms = jnp.mean(o * o, axis=0, keepdims=True)
        on = o * lax.rsqrt(ms + EPS) * sub_ref[...] * (1.0 - lambda_init)
        gate = gt_ref[0, :, pl.ds(q0, tq)].astype(F32)
        ot_ref[0, :, pl.ds(q0, tq)] = (on * _silu(gate)).astype(BF16)
        return carry

    load_queries(0)
    scores(buf_a, 0)
    lax.fori_loop(0, n_q_tiles, query_tile, 0)


def _out_proj_t_kernel(ot_ref, w_ref, x_ref, g_ref, xo_ref):
    g = g_ref[...]
    for c in range(PROJ_TM // OUT_ROWS):
        rows = slice(c * OUT_ROWS, (c + 1) * OUT_ROWS)
        y = lax.dot_general(ot_ref[0, :, rows], w_ref[...], TN_DIMS, preferred_element_type=F32)
        ms = jnp.mean(y * y, axis=-1, keepdims=True)
        xo_ref[rows, :] = x_ref[rows, :] + y * lax.rsqrt(ms + EPS) * g


def _kv_proj(x2, batch, seq, kv_g, wk, wvt, rope_tabs):
    tokens = x2.shape[0]
    cos, sin = rope_tabs
    nst = seq // PROJ_TM
    return pl.pallas_call(
        _kv_proj_kernel,
        out_shape=(jax.ShapeDtypeStruct((tokens, DIFF_QK_WIDTH), BF16),
                   jax.ShapeDtypeStruct((batch, DIFF_HEADS * ATT_VT_ROWS, seq), BF16)),
        grid=(batch, nst),
        in_specs=[
            pl.BlockSpec((PROJ_TM, D_MODEL), lambda b, i: (b * nst + i, 0)),
            pl.BlockSpec((1, D_MODEL), lambda b, i: (0, 0)),
            pl.BlockSpec((D_MODEL, DIFF_QK_WIDTH), lambda b, i: (0, 0)),
            pl.BlockSpec((DIFF_WIDTH, D_MODEL), lambda b, i: (0, 0)),
            pl.BlockSpec((PROJ_TM, LANES), lambda b, i: (i, 0)),
            pl.BlockSpec((PROJ_TM, LANES), lambda b, i: (i, 0)),
        ],
        out_specs=(pl.BlockSpec((PROJ_TM, DIFF_QK_WIDTH), lambda b, i: (b * nst + i, 0)),
                   pl.BlockSpec((1, DIFF_HEADS * ATT_VT_ROWS, PROJ_TM), lambda b, i: (b, 0, i))),
        compiler_params=_params(("arbitrary", "arbitrary")),
        name="kv_proj",
    )(x2, kv_g, wk, wvt, cos, sin)


def _diff_layer(x2, batch, seq, layer, pre_g, post_g, wt_in, w_out, k_sh, vt_sh, rope_t_tabs,
                sub_g, lq1, lk1, lq2, lk2):
    tokens = x2.shape[0]
    cos_t, sin_t = rope_t_tabs
    nst = seq // PROJ_TM
    qt, gt = pl.pallas_call(
        _q_proj_kernel,
        out_shape=(jax.ShapeDtypeStruct((batch, DIFF_QK_WIDTH, seq), BF16),
                   jax.ShapeDtypeStruct((batch, DIFF_WIDTH, seq), BF16)),
        grid=(batch, nst),
        in_specs=[
            pl.BlockSpec((PROJ_TM, D_MODEL), lambda b, i: (b * nst + i, 0)),
            pl.BlockSpec((1, D_MODEL), lambda b, i: (0, 0)),
            pl.BlockSpec((DIFF_QK_WIDTH + DIFF_WIDTH, D_MODEL), lambda b, i: (0, 0)),
            pl.BlockSpec((DIFF_HEAD_DIM // 2, PROJ_TM), lambda b, i: (0, i)),
            pl.BlockSpec((DIFF_HEAD_DIM // 2, PROJ_TM), lambda b, i: (0, i)),
        ],
        out_specs=(pl.BlockSpec((1, DIFF_QK_WIDTH, PROJ_TM), lambda b, i: (b, 0, i)),
                   pl.BlockSpec((1, DIFF_WIDTH, PROJ_TM), lambda b, i: (b, 0, i))),
        compiler_params=_params(("arbitrary", "arbitrary")),
        name="q_proj",
    )(x2, pre_g, wt_in, cos_t, sin_t)

    lambda_init = 0.8 - 0.6 * math.exp(-0.3 * layer)
    nq = seq // ATT_TQ
    k3 = k_sh.reshape(batch, seq, DIFF_QK_WIDTH)
    lam_spec = pl.BlockSpec((1, DIFF_HEAD_DIM), lambda b, h: (0, 0))
    ot = pl.pallas_call(
        functools.partial(_diff_attn_kernel, lambda_init=lambda_init, n_q_tiles=nq),
        out_shape=jax.ShapeDtypeStruct((batch, DIFF_WIDTH, seq), BF16),
        grid=(batch, DIFF_HEADS),
        in_specs=[
            pl.BlockSpec((1, 2 * DIFF_HEAD_DIM, seq), lambda b, h: (b, h, 0)),
            pl.BlockSpec((1, seq, 2 * DIFF_HEAD_DIM), lambda b, h: (b, 0, h)),
            pl.BlockSpec((1, ATT_VT_ROWS, seq), lambda b, h: (b, h, 0)),
            pl.BlockSpec((1, DIFF_V_DIM, seq), lambda b, h: (b, h, 0)),
            pl.BlockSpec((DIFF_V_DIM, 1), lambda b, h: (0, 0)),
            lam_spec, lam_spec, lam_spec, lam_spec,
        ],
        out_specs=pl.BlockSpec((1, DIFF_V_DIM, seq), lambda b, h: (b, h, 0)),
        scratch_shapes=[
            pltpu.VMEM((2 * DIFF_HEAD_DIM, 2 * ATT_TQ), BF16),
            pltpu.VMEM((DIFF_V_DIM + ATT_ONES_ROWS, 2 * ATT_TQ), F32),
            pltpu.VMEM((1, 2 * ATT_TQ), F32),
            pltpu.VMEM((ATT_TK, 2 * ATT_TQ), BF16),
            pltpu.VMEM((ATT_TK, 2 * ATT_TQ), BF16),
            pltpu.VMEM((ATT_TK, 2 * ATT_TQ), BF16),
            pltpu.VMEM((1, 2 * ATT_TQ), F32),
            pltpu.VMEM((1, 2 * ATT_TQ), F32),
            pltpu.VMEM((1, 2 * ATT_TQ), F32),
        ],
        compiler_params=_params(("arbitrary", "arbitrary")),
        name="diff_attn",
    )(qt, k3, vt_sh, gt, sub_g, lq1, lk1, lq2, lk2)

    return pl.pallas_call(
        _out_proj_t_kernel,
        out_shape=jax.ShapeDtypeStruct((tokens, D_MODEL), F32),
        grid=(batch, nst),
        in_specs=[
            pl.BlockSpec((1, DIFF_WIDTH, PROJ_TM), lambda b, i: (b, 0, i)),
            pl.BlockSpec((DIFF_WIDTH, D_MODEL), lambda b, i: (0, 0)),
            pl.BlockSpec((PROJ_TM, D_MODEL), lambda b, i: (b * nst + i, 0)),
            pl.BlockSpec((1, D_MODEL), lambda b, i: (0, 0)),
        ],
        out_specs=pl.BlockSpec((PROJ_TM, D_MODEL), lambda b, i: (b * nst + i, 0)),
        compiler_params=_params(("arbitrary", "arbitrary")),
        name="diff_out_proj",
    )(ot, w_out, x2, post_g)


def kernel(x, pre_norm, post_norm, w_in_a, w_out_a, kv_norm, w_kv, w_in_b,
           lam_q1, lam_k1, lam_q2, lam_k2, sub_norm_b, w_out_b):
    batch, seq, _ = x.shape
    x2 = x.reshape(batch * seq, D_MODEL)

    ret_rope = _rope_tables_rows(seq, RET_QK_DIM)
    ret_tabs = _retention_tables()
    for layer in range(N_A_LAYERS):
        x2 = _retention_layer(
            x2, batch, seq,
            pre_norm[layer].reshape(1, D_MODEL), post_norm[layer].reshape(1, D_MODEL),
            w_in_a[layer].astype(BF16), w_out_a[layer].astype(BF16), ret_rope, ret_tabs)

    diff_rope = _rope_tables_rows(seq, DIFF_HEAD_DIM)
    k_sh, vt_sh = _kv_proj(
        x2, batch, seq, kv_norm.reshape(1, D_MODEL),
        w_kv[:, :DIFF_QK_WIDTH].astype(BF16), w_kv[:, DIFF_QK_WIDTH:].T.astype(BF16), diff_rope)

    half = DIFF_HEAD_DIM // 2
    inv = ROPE_THETA ** (-jnp.arange(half, dtype=F32) / half)
    ang_t = inv[:, None] * jnp.arange(seq, dtype=F32)[None, :]
    rope_t = (jnp.cos(ang_t), jnp.sin(ang_t))
    for layer in range(N_A_LAYERS, DEPTH):
        j = layer - N_A_LAYERS
        x2 = _diff_layer(
            x2, batch, seq, layer,
            pre_norm[layer].reshape(1, D_MODEL), post_norm[layer].reshape(1, D_MODEL),
            w_in_b[j].T.astype(BF16), w_out_b[j].astype(BF16), k_sh, vt_sh, rope_t,
            sub_norm_b[j].reshape(DIFF_V_DIM, 1),
            lam_q1[j].reshape(1, DIFF_HEAD_DIM), lam_k1[j].reshape(1, DIFF_HEAD_DIM),
            lam_q2[j].reshape(1, DIFF_HEAD_DIM), lam_k2[j].reshape(1, DIFF_HEAD_DIM))
    return x2.reshape(batch, seq, D_MODEL)
```

```python
import functools
import math

import jax
import jax.numpy as jnp
from jax import lax
from jax.experimental import pallas as pl
from jax.experimental.pallas import tpu as pltpu

F32 = jnp.float32
BF16 = jnp.bfloat16

D_MODEL = 1024
DEPTH = 4
N_A_LAYERS = DEPTH // 2
CHUNK = 64
EPS = 1e-6
ROPE_THETA = 10000.0

RET_HEADS = 8
RET_QK_DIM = 128
RET_V_DIM = 256
RET_QK_WIDTH = RET_HEADS * RET_QK_DIM
RET_WIDTH = RET_HEADS * RET_V_DIM
RET_IN_WIDTH = 2 * RET_QK_WIDTH + 2 * RET_WIDTH

DIFF_HEADS = 8
DIFF_HEAD_DIM = 64
DIFF_V_DIM = 128
DIFF_QK_WIDTH = DIFF_HEADS * 2 * DIFF_HEAD_DIM
DIFF_WIDTH = DIFF_HEADS * DIFF_V_DIM

LANES = 128
VMEM_LIMIT = 56 * 1024 * 1024

PROJ_TM = 512
PROJ_ROWS = 512
OUT_ROWS = 256
RET_BLOCK = 1024
RET_CHUNK = 256
ATT_TQ = 512
ATT_TK = 512
FP8 = jnp.float8_e4m3fn
FP8_TARGET = 256.0
FP8_MIN_AMAX = 2.0 ** -30
ATT_ONES_ROWS = 16
ATT_VT_ROWS = DIFF_V_DIM + ATT_ONES_ROWS
MASK_NEG = -(2.0 ** 100)

NT_DIMS = (((1,), (1,)), ((), ()))
TN_DIMS = (((0,), (0,)), ((), ()))


def _params(sem):
    return pltpu.CompilerParams(dimension_semantics=sem, vmem_limit_bytes=VMEM_LIMIT)


def _normed(x_ref, g_ref):
    x = x_ref[...]
    ms = jnp.mean(x * x, axis=-1, keepdims=True)
    return (x * lax.rsqrt(ms + EPS) * g_ref[...]).astype(BF16)


def _silu(g):
    return g / (1.0 + jnp.exp(-g))


def _ret_in_proj_kernel(x_ref, g_ref, w_ref, cos_ref, sin_ref, o_ref):
    h = _normed(x_ref, g_ref)
    cos = cos_ref[...]
    sin = sin_ref[...]
    k_scale = RET_QK_DIM ** -0.5
    for j in range(RET_IN_WIDTH // D_MODEL):
        cols = slice(j * D_MODEL, (j + 1) * D_MODEL)
        r = jnp.dot(h, w_ref[:, cols], preferred_element_type=F32)
        if j < 2:
            for s in range(RET_HEADS):
                xs = r[:, s * LANES:(s + 1) * LANES]
                y = xs * cos + pltpu.roll(xs, RET_QK_DIM // 2, 1) * sin
                if j == 1:
                    y = y * k_scale
                o_ref[:, j * D_MODEL + s * LANES:j * D_MODEL + (s + 1) * LANES] = y.astype(BF16)
        else:
            o_ref[:, cols] = r.astype(BF16)


def _retention_kernel(q_ref, k_ref, v_ref, gate_ref, dm_ref, qd_ref, kd_ref, o_ref, state_ref):
    @pl.when(pl.program_id(2) == 0)
    def _():
        state_ref[...] = jnp.zeros_like(state_ref)

    dm = dm_ref[0]
    qd = qd_ref[0]
    kd = kd_ref[0]
    cdec = qd[RET_CHUNK - 1:RET_CHUNK, :]
    for i in range(RET_BLOCK // RET_CHUNK):
        rows = slice(i * RET_CHUNK, (i + 1) * RET_CHUNK)
        q = q_ref[rows, :]
        k = k_ref[rows, :]
        v = v_ref[rows, :]
        s = lax.dot_general(q, k, NT_DIMS, preferred_element_type=F32) * dm
        st = state_ref[...]
        o = jnp.dot(s.astype(BF16), v, preferred_element_type=F32)
        o = o + qd * jnp.dot(q, st.astype(BF16), preferred_element_type=F32)
        kdk = (k.astype(F32) * kd).astype(BF16)
        state_ref[...] = st * cdec + lax.dot_general(kdk, v, TN_DIMS, preferred_element_type=F32)
        mu = jnp.mean(o, axis=-1, keepdims=True)
        d = o - mu
        var = jnp.mean(d * d, axis=-1, keepdims=True)
        on = d * lax.rsqrt(var + EPS)
        o_ref[rows, :] = (on * _silu(gate_ref[rows, :].astype(F32))).astype(BF16)


def _out_proj_kernel(o_ref, w_ref, x_ref, g_ref, xo_ref):
    g = g_ref[...]
    for c in range(PROJ_TM // OUT_ROWS):
        rows = slice(c * OUT_ROWS, (c + 1) * OUT_ROWS)
        y = jnp.dot(o_ref[rows, :], w_ref[...], preferred_element_type=F32)
        ms = jnp.mean(y * y, axis=-1, keepdims=True)
        xo_ref[rows, :] = x_ref[rows, :] + y * lax.rsqrt(ms + EPS) * g


def _retention_tables():
    h = jnp.arange(RET_HEADS, dtype=F32)
    log_gamma = jnp.log1p(-jnp.exp2(-5.0 - h))
    pos = jnp.arange(RET_CHUNK, dtype=F32)
    diff = pos[:, None] - pos[None, :]
    dm = jnp.where(diff[None] >= 0,
                   jnp.exp(jnp.maximum(diff, 0.0)[None] * log_gamma[:, None, None]), 0.0)
    qd = jnp.exp((pos[None, :] + 1.0) * log_gamma[:, None])
    kd = jnp.exp((RET_CHUNK - 1.0 - pos[None, :]) * log_gamma[:, None])
    qd = jnp.broadcast_to(qd[:, :, None], (RET_HEADS, RET_CHUNK, RET_V_DIM))
    kd = jnp.broadcast_to(kd[:, :, None], (RET_HEADS, RET_CHUNK, RET_QK_DIM))
    return dm, qd, kd


def _rope_tables_rows(seq, dim):
    half = dim // 2
    inv = ROPE_THETA ** (-jnp.arange(half, dtype=F32) / half)
    ang = jnp.arange(seq, dtype=F32)[:, None] * inv[None, :]
    cos = jnp.cos(ang)
    sin = jnp.sin(ang)
    reps = LANES // dim
    cos_full = jnp.tile(jnp.concatenate([cos, cos], axis=1), (1, reps))
    sin_signed = jnp.tile(jnp.concatenate([-sin, sin], axis=1), (1, reps))
    return cos_full, sin_signed


def _retention_layer(x2, batch, seq, pre_g, post_g, w_in, w_out, rope_tabs, ret_tabs):
    tokens = x2.shape[0]
    cos, sin = rope_tabs
    n_seq_tiles = seq // PROJ_TM
    proj = pl.pallas_call(
        _ret_in_proj_kernel,
        out_shape=jax.ShapeDtypeStruct((tokens, RET_IN_WIDTH), BF16),
        grid=(tokens // PROJ_TM,),
        in_specs=[
            pl.BlockSpec((PROJ_TM, D_MODEL), lambda i: (i, 0)),
            pl.BlockSpec((1, D_MODEL), lambda i: (0, 0)),
            pl.BlockSpec((D_MODEL, RET_IN_WIDTH), lambda i: (0, 0)),
            pl.BlockSpec((PROJ_TM, LANES), lambda i: (i % n_seq_tiles, 0)),
            pl.BlockSpec((PROJ_TM, LANES), lambda i: (i % n_seq_tiles, 0)),
        ],
        out_specs=pl.BlockSpec((PROJ_TM, RET_IN_WIDTH), lambda i: (i, 0)),
        compiler_params=_params(("arbitrary",)),
        name="ret_in_proj",
    )(x2, pre_g, w_in, cos, sin)

    dm, qd, kd = ret_tabs
    nblk = seq // RET_BLOCK
    v_off = 2 * RET_QK_WIDTH // RET_V_DIM
    g_off = (2 * RET_QK_WIDTH + RET_WIDTH) // RET_V_DIM
    o = pl.pallas_call(
        _retention_kernel,
        out_shape=jax.ShapeDtypeStruct((tokens, RET_WIDTH), BF16),
        grid=(batch, RET_HEADS, nblk),
        in_specs=[
            pl.BlockSpec((RET_BLOCK, RET_QK_DIM), lambda b, h, c: (b * nblk + c, h)),
            pl.BlockSpec((RET_BLOCK, RET_QK_DIM), lambda b, h, c: (b * nblk + c, RET_HEADS + h)),
            pl.BlockSpec((RET_BLOCK, RET_V_DIM), lambda b, h, c: (b * nblk + c, v_off + h)),
            pl.BlockSpec((RET_BLOCK, RET_V_DIM), lambda b, h, c: (b * nblk + c, g_off + h)),
            pl.BlockSpec((1, RET_CHUNK, RET_CHUNK), lambda b, h, c: (h, 0, 0)),
            pl.BlockSpec((1, RET_CHUNK, RET_V_DIM), lambda b, h, c: (h, 0, 0)),
            pl.BlockSpec((1, RET_CHUNK, RET_QK_DIM), lambda b, h, c: (h, 0, 0)),
        ],
        out_specs=pl.BlockSpec((RET_BLOCK, RET_V_DIM), lambda b, h, c: (b * nblk + c, h)),
        scratch_shapes=[pltpu.VMEM((RET_QK_DIM, RET_V_DIM), F32)],
        compiler_params=_params(("arbitrary", "arbitrary", "arbitrary")),
        name="retention",
    )(proj, proj, proj, proj, dm, qd, kd)

    return pl.pallas_call(
        _out_proj_kernel,
        out_shape=jax.ShapeDtypeStruct((tokens, D_MODEL), F32),
        grid=(tokens // PROJ_TM,),
        in_specs=[
            pl.BlockSpec((PROJ_TM, RET_WIDTH), lambda i: (i, 0)),
            pl.BlockSpec((RET_WIDTH, D_MODEL), lambda i: (0, 0)),
            pl.BlockSpec((PROJ_TM, D_MODEL), lambda i: (i, 0)),
            pl.BlockSpec((1, D_MODEL), lambda i: (0, 0)),
        ],
        out_specs=pl.BlockSpec((PROJ_TM, D_MODEL), lambda i: (i, 0)),
        compiler_params=_params(("arbitrary",)),
        name="ret_out_proj",
    )(o, w_out, x2, post_g)


def _kv_proj_kernel(x_ref, g_ref, wk_ref, wvt_ref, cos_ref, sin_ref, k_ref, vt_ref):
    h = _normed(x_ref, g_ref)
    cos = cos_ref[...]
    sin = sin_ref[...]
    kf = jnp.dot(h, wk_ref[...], preferred_element_type=F32)
    lane = lax.broadcasted_iota(jnp.int32, (PROJ_TM, LANES), 1)
    first_half = (lane & (DIFF_HEAD_DIM - 1)) < (DIFF_HEAD_DIM // 2)
    for s in range(DIFF_QK_WIDTH // LANES):
        xs = kf[:, s * LANES:(s + 1) * LANES]
        rot = jnp.where(first_half,
                        pltpu.roll(xs, LANES - DIFF_HEAD_DIM // 2, 1),
                        pltpu.roll(xs, DIFF_HEAD_DIM // 2, 1))
        k_ref[:, s * LANES:(s + 1) * LANES] = (xs * cos + rot * sin).astype(BF16)
    vt = lax.dot_general(wvt_ref[...], h, NT_DIMS, preferred_element_type=F32)
    ones_rows = jnp.ones((ATT_ONES_ROWS, PROJ_TM), BF16)
    for hd in range(DIFF_HEADS):
        r0 = hd * ATT_VT_ROWS
        vt_ref[0, r0:r0 + DIFF_V_DIM, :] = vt[hd * DIFF_V_DIM:(hd + 1) * DIFF_V_DIM, :].astype(BF16)
        vt_ref[0, r0 + DIFF_V_DIM:r0 + ATT_VT_ROWS, :] = ones_rows


def _q_proj_kernel(x_ref, g_ref, wt_ref, cos_ref, sin_ref, qt_ref, gt_ref):
    h = _normed(x_ref, g_ref)
    cos = cos_ref[...]
    sin = sin_ref[...]
    scale = DIFF_HEAD_DIM ** -0.5 * math.log2(math.e)
    half = DIFF_HEAD_DIM // 2
    for c in range((DIFF_QK_WIDTH + DIFF_WIDTH) // PROJ_ROWS):
        c0 = c * PROJ_ROWS
        pt = lax.dot_general(wt_ref[c0:c0 + PROJ_ROWS, :], h, NT_DIMS,
                             preferred_element_type=F32)
        if c0 < DIFF_QK_WIDTH:
            for grp in range(PROJ_ROWS // DIFF_HEAD_DIM):
                r0 = grp * DIFF_HEAD_DIM
                x1 = pt[r0:r0 + half, :]
                x2 = pt[r0 + half:r0 + DIFF_HEAD_DIM, :]
                qt_ref[0, c0 + r0:c0 + r0 + half, :] = ((x1 * cos - x2 * sin) * scale).astype(BF16)
                qt_ref[0, c0 + r0 + half:c0 + r0 + DIFF_HEAD_DIM, :] = (
                    (x1 * sin + x2 * cos) * scale).astype(BF16)
        else:
            g0 = c0 - DIFF_QK_WIDTH
            gt_ref[0, g0:g0 + PROJ_ROWS, :] = pt.astype(BF16)


def _pow2_scale(amax):
    e = jnp.floor(jnp.log2(jnp.maximum(amax, FP8_MIN_AMAX))) + (1.0 - math.log2(FP8_TARGET))
    return jnp.exp2(e)


def _hi_lo(x):
    hi = x.astype(FP8).astype(F32)
    lo = (x - hi).astype(FP8).astype(F32)
    return hi, lo


def _diff_attn_kernel(qt_ref, k_ref, vt_ref, gt_ref, sub_ref, lq1_ref, lk1_ref, lq2_ref, lk2_ref,
                      ot_ref, k1a_ref, k2a_ref, q1a_ref, q2a_ref, acc_ref, m_ref,
                      sa_ref, sb_ref, sc_ref, ma_ref, mb_ref, mc_ref, *, lambda_init, n_q_tiles):
    tq, tk = ATT_TQ, ATT_TK
    dh = DIFF_HEAD_DIM
    dv = DIFF_V_DIM
    seq = k_ref.shape[1]
    maps = ((k1a_ref, q1a_ref, slice(0, tq)), (k2a_ref, q2a_ref, slice(tq, 2 * tq)))
    buf_a, buf_b, buf_c = (sa_ref, ma_ref), (sb_ref, mb_ref), (sc_ref, mc_ref)
    lam = (jnp.exp(jnp.sum(lq1_ref[...] * lk1_ref[...], axis=-1, keepdims=True))
           - jnp.exp(jnp.sum(lq2_ref[...] * lk2_ref[...], axis=-1, keepdims=True))
           + lambda_init)

    def absmax(x):
        return jnp.max(jnp.max(jnp.abs(x), axis=0, keepdims=True), axis=1, keepdims=True)

    sk = _pow2_scale(absmax(k_ref[0]).astype(F32))
    sq = _pow2_scale(absmax(qt_ref[0]).astype(F32))
    inv_sk = 1.0 / sk
    inv_sq = 1.0 / sq
    c = sk * sq
    c16 = c.astype(BF16)

    first_half = lax.broadcasted_iota(jnp.int32, (tk, 2 * dh), 1) < dh

    def split_keys(r, carry):
        rows = pl.ds(pl.multiple_of(r * tk, tk), tk)
        hi, lo = _hi_lo(k_ref[0, rows, :].astype(F32) * inv_sk)
        hi_sw = pltpu.roll(hi, dh, 1)
        lo_sw = pltpu.roll(lo, dh, 1)
        k1a_ref[rows, 0:2 * dh] = jnp.where(first_half, hi, lo_sw).astype(FP8)
        k1a_ref[rows, 2 * dh:4 * dh] = jnp.where(first_half, hi, 0.0).astype(FP8)
        k2a_ref[rows, 0:2 * dh] = jnp.where(first_half, hi_sw, lo).astype(FP8)
        k2a_ref[rows, 2 * dh:4 * dh] = jnp.where(first_half, hi_sw, 0.0).astype(FP8)
        return carry

    lax.fori_loop(0, seq // tk, split_keys, 0)

    def load_queries(qi):
        q0 = pl.multiple_of(qi * tq, tq)
        hi, lo = _hi_lo(qt_ref[0, :, pl.ds(q0, tq)].astype(F32) * inv_sq)
        zeros = jnp.zeros((dh, tq), FP8)
        for mp, (_, qa_ref, _) in enumerate(maps):
            rows = slice(mp * dh, (mp + 1) * dh)
            qa_ref[0:dh, :] = hi[rows, :].astype(FP8)
            qa_ref[dh:2 * dh, :] = hi[rows, :].astype(FP8)
            qa_ref[2 * dh:3 * dh, :] = lo[rows, :].astype(FP8)
            qa_ref[3 * dh:4 * dh, :] = zeros

    def scores(buf, j):
        s_ref, tile_max_ref = buf
        kk = pl.multiple_of(j * tk, tk)
        for ka_ref, qa_ref, cols in maps:
            s = jnp.dot(ka_ref[pl.ds(kk, tk), :], qa_ref[...], preferred_element_type=F32)
            s = s.astype(BF16)
            s_ref[:, cols] = s
            tile_max_ref[:, cols] = jnp.max(s, axis=0, keepdims=True).astype(F32)

    def consume(buf, j, diagonal):
        s_ref, tile_max_ref = buf
        kk = pl.multiple_of(j * tk, tk)
        vt = vt_ref[0, :, pl.ds(kk, tk)]
        for _, _, cols in maps:
            m_old = m_ref[:, cols]
            if diagonal:
                s = s_ref[:, cols].astype(F32)
                key_chunk = lax.broadcasted_iota(jnp.int32, s.shape, 0) // CHUNK
                qry_chunk = lax.broadcasted_iota(jnp.int32, s.shape, 1) // CHUNK
                s = jnp.where(key_chunk <= qry_chunk, s, MASK_NEG)
                m_new = jnp.maximum(m_old, jnp.max(s, axis=0, keepdims=True))
                p = jnp.exp2((s - m_new) * c).astype(BF16)
            else:
                m_new = jnp.maximum(m_old, tile_max_ref[:, cols])
                p = jnp.exp2((s_ref[:, cols] - m_new.astype(BF16)) * c16)
            alpha = jnp.exp2((m_old - m_new) * c)
            pv = jnp.dot(vt, p, preferred_element_type=F32)
            acc_ref[:, cols] = alpha * acc_ref[:, cols] + pv
            m_ref[:, cols] = m_new

    def steady(i, carry):
        j = 3 * i
        scores(buf_b, j + 1)
        consume(buf_a, j, False)
        scores(buf_c, j + 2)
        consume(buf_b, j + 1, False)
        scores(buf_a, j + 3)
        consume(buf_c, j + 2, False)
        return carry

    def query_tile(qi, carry):
        m_ref[...] = jnp.full(m_ref.shape, MASK_NEG, F32)
        acc_ref[...] = jnp.zeros(acc_ref.shape, F32)
        trips = qi // 3
        lax.fori_loop(0, trips, steady, 0)
        j0 = 3 * trips
        left = qi + 1 - j0

        @pl.when(left == 1)
        def _():
            consume(buf_a, qi, True)

        @pl.when(left == 2)
        def _():
            scores(buf_b, qi)
            consume(buf_a, j0, False)
            consume(buf_b, qi, True)

        @pl.when(left == 3)
        def _():
            scores(buf_b, j0 + 1)
            consume(buf_a, j0, False)
            scores(buf_c, qi)
            consume(buf_b, j0 + 1, False)
            consume(buf_c, qi, True)

        q0 = pl.multiple_of(qi * tq, tq)
        a = acc_ref[0:dv, :] * (1.0 / acc_ref[dv:dv + 1, :])
        load_queries(jnp.minimum(qi + 1, n_q_tiles - 1))
        scores(buf_a, 0)
        o = a[:, 0:tq] - lam * a[:, tq:2 * tq]
        ms = jnp.mean(o * o, axis=0, keepdims=True)
        on = o * lax.rsqrt(ms + EPS) * sub_ref[...] * (1.0 - lambda_init)
        gate = gt_ref[0, :, pl.ds(q0, tq)].astype(F32)
        ot_ref[0, :, pl.ds(q0, tq)] = (on * _silu(gate)).astype(BF16)
        return carry

    load_queries(0)
    scores(buf_a, 0)
    lax.fori_loop(0, n_q_tiles, query_tile, 0)


def _out_proj_t_kernel(ot_ref, w_ref, x_ref, g_ref, xo_ref):
    g = g_ref[...]
    for c in range(PROJ_TM // OUT_ROWS):
        rows = slice(c * OUT_ROWS, (c + 1) * OUT_ROWS)
        y = lax.dot_general(ot_ref[0, :, rows], w_ref[...], TN_DIMS, preferred_element_type=F32)
        ms = jnp.mean(y * y, axis=-1, keepdims=True)
        xo_ref[rows, :] = x_ref[rows, :] + y * lax.rsqrt(ms + EPS) * g


def _kv_proj(x2, batch, seq, kv_g, wk, wvt, rope_tabs):
    tokens = x2.shape[0]
    cos, sin = rope_tabs
    nst = seq // PROJ_TM
    return pl.pallas_call(
        _kv_proj_kernel,
        out_shape=(jax.ShapeDtypeStruct((tokens, DIFF_QK_WIDTH), BF16),
                   jax.ShapeDtypeStruct((batch, DIFF_HEADS * ATT_VT_ROWS, seq), BF16)),
        grid=(batch, nst),
        in_specs=[
            pl.BlockSpec((PROJ_TM, D_MODEL), lambda b, i: (b * nst + i, 0)),
            pl.BlockSpec((1, D_MODEL), lambda b, i: (0, 0)),
            pl.BlockSpec((D_MODEL, DIFF_QK_WIDTH), lambda b, i: (0, 0)),
            pl.BlockSpec((DIFF_WIDTH, D_MODEL), lambda b, i: (0, 0)),
            pl.BlockSpec((PROJ_TM, LANES), lambda b, i: (i, 0)),
            pl.BlockSpec((PROJ_TM, LANES), lambda b, i: (i, 0)),
        ],
        out_specs=(pl.BlockSpec((PROJ_TM, DIFF_QK_WIDTH), lambda b, i: (b * nst + i, 0)),
                   pl.BlockSpec((1, DIFF_HEADS * ATT_VT_ROWS, PROJ_TM), lambda b, i: (b, 0, i))),
        compiler_params=_params(("arbitrary", "arbitrary")),
        name="kv_proj",
    )(x2, kv_g, wk, wvt, cos, sin)


def _diff_layer(x2, batch, seq, layer, pre_g, post_g, wt_in, w_out, k_sh, vt_sh, rope_t_tabs,
                sub_g, lq1, lk1, lq2, lk2):
    tokens = x2.shape[0]
    cos_t, sin_t = rope_t_tabs
    nst = seq // PROJ_TM
    qt, gt = pl.pallas_call(
        _q_proj_kernel,
        out_shape=(jax.ShapeDtypeStruct((batch, DIFF_QK_WIDTH, seq), BF16),
                   jax.ShapeDtypeStruct((batch, DIFF_WIDTH, seq), BF16)),
        grid=(batch, nst),
        in_specs=[
            pl.BlockSpec((PROJ_TM, D_MODEL), lambda b, i: (b * nst + i, 0)),
            pl.BlockSpec((1, D_MODEL), lambda b, i: (0, 0)),
            pl.BlockSpec((DIFF_QK_WIDTH + DIFF_WIDTH, D_MODEL), lambda b, i: (0, 0)),
            pl.BlockSpec((DIFF_HEAD_DIM // 2, PROJ_TM), lambda b, i: (0, i)),
            pl.BlockSpec((DIFF_HEAD_DIM // 2, PROJ_TM), lambda b, i: (0, i)),
        ],
        out_specs=(pl.BlockSpec((1, DIFF_QK_WIDTH, PROJ_TM), lambda b, i: (b, 0, i)),
                   pl.BlockSpec((1, DIFF_WIDTH, PROJ_TM), lambda b, i: (b, 0, i))),
        compiler_params=_params(("arbitrary", "arbitrary")),
        name="q_proj",
    )(x2, pre_g, wt_in, cos_t, sin_t)

    lambda_init = 0.8 - 0.6 * math.exp(-0.3 * layer)
    nq = seq // ATT_TQ
    k3 = k_sh.reshape(batch, seq, DIFF_QK_WIDTH)
    lam_spec = pl.BlockSpec((1, DIFF_HEAD_DIM), lambda b, h: (0, 0))
    ot = pl.pallas_call(
        functools.partial(_diff_attn_kernel, lambda_init=lambda_init, n_q_tiles=nq),
        out_shape=jax.ShapeDtypeStruct((batch, DIFF_WIDTH, seq), BF16),
        grid=(batch, DIFF_HEADS),
        in_specs=[
            pl.BlockSpec((1, 2 * DIFF_HEAD_DIM, seq), lambda b, h: (b, h, 0)),
            pl.BlockSpec((1, seq, 2 * DIFF_HEAD_DIM), lambda b, h: (b, 0, h)),
            pl.BlockSpec((1, ATT_VT_ROWS, seq), lambda b, h: (b, h, 0)),
            pl.BlockSpec((1, DIFF_V_DIM, seq), lambda b, h: (b, h, 0)),
            pl.BlockSpec((DIFF_V_DIM, 1), lambda b, h: (0, 0)),
            lam_spec, lam_spec, lam_spec, lam_spec,
        ],
        out_specs=pl.BlockSpec((1, DIFF_V_DIM, seq), lambda b, h: (b, h, 0)),
        scratch_shapes=[
            pltpu.VMEM((seq, 4 * DIFF_HEAD_DIM), FP8),
            pltpu.VMEM((seq, 4 * DIFF_HEAD_DIM), FP8),
            pltpu.VMEM((4 * DIFF_HEAD_DIM, ATT_TQ), FP8),
            pltpu.VMEM((4 * DIFF_HEAD_DIM, ATT_TQ), FP8),
            pltpu.VMEM((DIFF_V_DIM + ATT_ONES_ROWS, 2 * ATT_TQ), F32),
            pltpu.VMEM((1, 2 * ATT_TQ), F32),
            pltpu.VMEM((ATT_TK, 2 * ATT_TQ), BF16),
            pltpu.VMEM((ATT_TK, 2 * ATT_TQ), BF16),
            pltpu.VMEM((ATT_TK, 2 * ATT_TQ), BF16),
            pltpu.VMEM((1, 2 * ATT_TQ), F32),
            pltpu.VMEM((1, 2 * ATT_TQ), F32),
            pltpu.VMEM((1, 2 * ATT_TQ), F32),
        ],
        compiler_params=_params(("arbitrary", "arbitrary")),
        name="diff_attn",
    )(qt, k3, vt_sh, gt, sub_g, lq1, lk1, lq2, lk2)

    return pl.pallas_call(
        _out_proj_t_kernel,
        out_shape=jax.ShapeDtypeStruct((tokens, D_MODEL), F32),
        grid=(batch, nst),
        in_specs=[
            pl.BlockSpec((1, DIFF_WIDTH, PROJ_TM), lambda b, i: (b, 0, i)),
            pl.BlockSpec((DIFF_WIDTH, D_MODEL), lambda b, i: (0, 0)),
            pl.BlockSpec((PROJ_TM, D_MODEL), lambda b, i: (b * nst + i, 0)),
            pl.BlockSpec((1, D_MODEL), lambda b, i: (0, 0)),
        ],
        out_specs=pl.BlockSpec((PROJ_TM, D_MODEL), lambda b, i: (b * nst + i, 0)),
        compiler_params=_params(("arbitrary", "arbitrary")),
        name="diff_out_proj",
    )(ot, w_out, x2, post_g)


def kernel(x, pre_norm, post_norm, w_in_a, w_out_a, kv_norm, w_kv, w_in_b,
           lam_q1, lam_k1, lam_q2, lam_k2, sub_norm_b, w_out_b):
    batch, seq, _ = x.shape
    x2 = x.reshape(batch * seq, D_MODEL)

    ret_rope = _rope_tables_rows(seq, RET_QK_DIM)
    ret_tabs = _retention_tables()
    for layer in range(N_A_LAYERS):
        x2 = _retention_layer(
            x2, batch, seq,
            pre_norm[layer].reshape(1, D_MODEL), post_norm[layer].reshape(1, D_MODEL),
            w_in_a[layer].astype(BF16), w_out_a[layer].astype(BF16), ret_rope, ret_tabs)

    diff_rope = _rope_tables_rows(seq, DIFF_HEAD_DIM)
    k_sh, vt_sh = _kv_proj(
        x2, batch, seq, kv_norm.reshape(1, D_MODEL),
        w_kv[:, :DIFF_QK_WIDTH].astype(BF16), w_kv[:, DIFF_QK_WIDTH:].T.astype(BF16), diff_rope)

    half = DIFF_HEAD_DIM // 2
    inv = ROPE_THETA ** (-jnp.arange(half, dtype=F32) / half)
    ang_t = inv[:, None] * jnp.arange(seq, dtype=F32)[None, :]
    rope_t = (jnp.cos(ang_t), jnp.sin(ang_t))
    for layer in range(N_A_LAYERS, DEPTH):
        j = layer - N_A_LAYERS
        x2 = _diff_layer(
            x2, batch, seq, layer,
            pre_norm[layer].reshape(1, D_MODEL), post_norm[layer].reshape(1, D_MODEL),
            w_in_b[j].T.astype(BF16), w_out_b[j].astype(BF16), k_sh, vt_sh, rope_t,
            sub_norm_b[j].reshape(DIFF_V_DIM, 1),
            lam_q1[j].reshape(1, DIFF_HEAD_DIM), lam_k1[j].reshape(1, DIFF_HEAD_DIM),
            lam_q2[j].reshape(1, DIFF_HEAD_DIM), lam_k2[j].reshape(1, DIFF_HEAD_DIM))
    return x2.reshape(batch, seq, D_MODEL)
```

```python
import functools
import math

import jax
import jax.numpy as jnp
from jax import lax
from jax.experimental import pallas as pl
from jax.experimental.pallas import tpu as pltpu

F32 = jnp.float32
BF16 = jnp.bfloat16

D_MODEL = 1024
DEPTH = 4
N_A_LAYERS = DEPTH // 2
CHUNK = 64
EPS = 1e-6
ROPE_THETA = 10000.0

RET_HEADS = 8
RET_QK_DIM = 128
RET_V_DIM = 256
RET_QK_WIDTH = RET_HEADS * RET_QK_DIM
RET_WIDTH = RET_HEADS * RET_V_DIM
RET_IN_WIDTH = 2 * RET_QK_WIDTH + 2 * RET_WIDTH

DIFF_HEADS = 8
DIFF_HEAD_DIM = 64
DIFF_V_DIM = 128
DIFF_QK_WIDTH = DIFF_HEADS * 2 * DIFF_HEAD_DIM
DIFF_WIDTH = DIFF_HEADS * DIFF_V_DIM

LANES = 128
SUBLANES = 8
VMEM_LIMIT = 56 * 1024 * 1024

PROJ_TM = 512
PROJ_ROWS = 512
OUT_ROWS = 256
RET_BLOCK = 1024
RET_CHUNK = 256
ATT_TQ = 512
ATT_TK = 512
FP8 = jnp.float8_e4m3fn
FP8_TARGET = 256.0
FP8_MIN_AMAX = 2.0 ** -30
ATT_K8_LANES = 8 * DIFF_HEAD_DIM
ATT_Q8_ROWS = 4 * DIFF_HEAD_DIM
ATT_ONES_ROWS = 16
ATT_VT_ROWS = DIFF_V_DIM + ATT_ONES_ROWS
MASK_NEG = -(2.0 ** 100)

NT_DIMS = (((1,), (1,)), ((), ()))
TN_DIMS = (((0,), (0,)), ((), ()))


def _params(sem):
    return pltpu.CompilerParams(dimension_semantics=sem, vmem_limit_bytes=VMEM_LIMIT)


def _normed(x_ref, g_ref):
    x = x_ref[...]
    ms = jnp.mean(x * x, axis=-1, keepdims=True)
    return (x * lax.rsqrt(ms + EPS) * g_ref[...]).astype(BF16)


def _silu(g):
    return g / (1.0 + jnp.exp(-g))


def _ret_in_proj_kernel(x_ref, g_ref, w_ref, cos_ref, sin_ref, o_ref):
    h = _normed(x_ref, g_ref)
    cos = cos_ref[...]
    sin = sin_ref[...]
    k_scale = RET_QK_DIM ** -0.5
    for j in range(RET_IN_WIDTH // D_MODEL):
        cols = slice(j * D_MODEL, (j + 1) * D_MODEL)
        r = jnp.dot(h, w_ref[:, cols], preferred_element_type=F32)
        if j < 2:
            for s in range(RET_HEADS):
                xs = r[:, s * LANES:(s + 1) * LANES]
                y = xs * cos + pltpu.roll(xs, RET_QK_DIM // 2, 1) * sin
                if j == 1:
                    y = y * k_scale
                o_ref[:, j * D_MODEL + s * LANES:j * D_MODEL + (s + 1) * LANES] = y.astype(BF16)
        else:
            o_ref[:, cols] = r.astype(BF16)


def _retention_kernel(q_ref, k_ref, v_ref, gate_ref, dm_ref, qd_ref, kd_ref, o_ref, state_ref):
    @pl.when(pl.program_id(2) == 0)
    def _():
        state_ref[...] = jnp.zeros_like(state_ref)

    dm = dm_ref[0]
    qd = qd_ref[0]
    kd = kd_ref[0]
    cdec = qd[RET_CHUNK - 1:RET_CHUNK, :]
    for i in range(RET_BLOCK // RET_CHUNK):
        rows = slice(i * RET_CHUNK, (i + 1) * RET_CHUNK)
        q = q_ref[rows, :]
        k = k_ref[rows, :]
        v = v_ref[rows, :]
        s = lax.dot_general(q, k, NT_DIMS, preferred_element_type=F32) * dm
        st = state_ref[...]
        o = jnp.dot(s.astype(BF16), v, preferred_element_type=F32)
        o = o + qd * jnp.dot(q, st.astype(BF16), preferred_element_type=F32)
        kdk = (k.astype(F32) * kd).astype(BF16)
        state_ref[...] = st * cdec + lax.dot_general(kdk, v, TN_DIMS, preferred_element_type=F32)
        mu = jnp.mean(o, axis=-1, keepdims=True)
        d = o - mu
        var = jnp.mean(d * d, axis=-1, keepdims=True)
        on = d * lax.rsqrt(var + EPS)
        o_ref[rows, :] = (on * _silu(gate_ref[rows, :].astype(F32))).astype(BF16)


def _out_proj_kernel(o_ref, w_ref, x_ref, g_ref, xo_ref):
    g = g_ref[...]
    for c in range(PROJ_TM // OUT_ROWS):
        rows = slice(c * OUT_ROWS, (c + 1) * OUT_ROWS)
        y = jnp.dot(o_ref[rows, :], w_ref[...], preferred_element_type=F32)
        ms = jnp.mean(y * y, axis=-1, keepdims=True)
        xo_ref[rows, :] = x_ref[rows, :] + y * lax.rsqrt(ms + EPS) * g


def _retention_tables():
    h = jnp.arange(RET_HEADS, dtype=F32)
    log_gamma = jnp.log1p(-jnp.exp2(-5.0 - h))
    pos = jnp.arange(RET_CHUNK, dtype=F32)
    diff = pos[:, None] - pos[None, :]
    dm = jnp.where(diff[None] >= 0,
                   jnp.exp(jnp.maximum(diff, 0.0)[None] * log_gamma[:, None, None]), 0.0)
    qd = jnp.exp((pos[None, :] + 1.0) * log_gamma[:, None])
    kd = jnp.exp((RET_CHUNK - 1.0 - pos[None, :]) * log_gamma[:, None])
    qd = jnp.broadcast_to(qd[:, :, None], (RET_HEADS, RET_CHUNK, RET_V_DIM))
    kd = jnp.broadcast_to(kd[:, :, None], (RET_HEADS, RET_CHUNK, RET_QK_DIM))
    return dm, qd, kd


def _rope_tables_rows(seq, dim):
    half = dim // 2
    inv = ROPE_THETA ** (-jnp.arange(half, dtype=F32) / half)
    ang = jnp.arange(seq, dtype=F32)[:, None] * inv[None, :]
    cos = jnp.cos(ang)
    sin = jnp.sin(ang)
    reps = LANES // dim
    cos_full = jnp.tile(jnp.concatenate([cos, cos], axis=1), (1, reps))
    sin_signed = jnp.tile(jnp.concatenate([-sin, sin], axis=1), (1, reps))
    return cos_full, sin_signed


def _retention_layer(x2, batch, seq, pre_g, post_g, w_in, w_out, rope_tabs, ret_tabs):
    tokens = x2.shape[0]
    cos, sin = rope_tabs
    n_seq_tiles = seq // PROJ_TM
    proj = pl.pallas_call(
        _ret_in_proj_kernel,
        out_shape=jax.ShapeDtypeStruct((tokens, RET_IN_WIDTH), BF16),
        grid=(tokens // PROJ_TM,),
        in_specs=[
            pl.BlockSpec((PROJ_TM, D_MODEL), lambda i: (i, 0)),
            pl.BlockSpec((1, D_MODEL), lambda i: (0, 0)),
            pl.BlockSpec((D_MODEL, RET_IN_WIDTH), lambda i: (0, 0)),
            pl.BlockSpec((PROJ_TM, LANES), lambda i: (i % n_seq_tiles, 0)),
            pl.BlockSpec((PROJ_TM, LANES), lambda i: (i % n_seq_tiles, 0)),
        ],
        out_specs=pl.BlockSpec((PROJ_TM, RET_IN_WIDTH), lambda i: (i, 0)),
        compiler_params=_params(("arbitrary",)),
        name="ret_in_proj",
    )(x2, pre_g, w_in, cos, sin)

    dm, qd, kd = ret_tabs
    nblk = seq // RET_BLOCK
    v_off = 2 * RET_QK_WIDTH // RET_V_DIM
    g_off = (2 * RET_QK_WIDTH + RET_WIDTH) // RET_V_DIM
    o = pl.pallas_call(
        _retention_kernel,
        out_shape=jax.ShapeDtypeStruct((tokens, RET_WIDTH), BF16),
        grid=(batch, RET_HEADS, nblk),
        in_specs=[
            pl.BlockSpec((RET_BLOCK, RET_QK_DIM), lambda b, h, c: (b * nblk + c, h)),
            pl.BlockSpec((RET_BLOCK, RET_QK_DIM), lambda b, h, c: (b * nblk + c, RET_HEADS + h)),
            pl.BlockSpec((RET_BLOCK, RET_V_DIM), lambda b, h, c: (b * nblk + c, v_off + h)),
            pl.BlockSpec((RET_BLOCK, RET_V_DIM), lambda b, h, c: (b * nblk + c, g_off + h)),
            pl.BlockSpec((1, RET_CHUNK, RET_CHUNK), lambda b, h, c: (h, 0, 0)),
            pl.BlockSpec((1, RET_CHUNK, RET_V_DIM), lambda b, h, c: (h, 0, 0)),
            pl.BlockSpec((1, RET_CHUNK, RET_QK_DIM), lambda b, h, c: (h, 0, 0)),
        ],
        out_specs=pl.BlockSpec((RET_BLOCK, RET_V_DIM), lambda b, h, c: (b * nblk + c, h)),
        scratch_shapes=[pltpu.VMEM((RET_QK_DIM, RET_V_DIM), F32)],
        compiler_params=_params(("arbitrary", "arbitrary", "arbitrary")),
        name="retention",
    )(proj, proj, proj, proj, dm, qd, kd)

    return pl.pallas_call(
        _out_proj_kernel,
        out_shape=jax.ShapeDtypeStruct((tokens, D_MODEL), F32),
        grid=(tokens // PROJ_TM,),
        in_specs=[
            pl.BlockSpec((PROJ_TM, RET_WIDTH), lambda i: (i, 0)),
            pl.BlockSpec((RET_WIDTH, D_MODEL), lambda i: (0, 0)),
            pl.BlockSpec((PROJ_TM, D_MODEL), lambda i: (i, 0)),
            pl.BlockSpec((1, D_MODEL), lambda i: (0, 0)),
        ],
        out_specs=pl.BlockSpec((PROJ_TM, D_MODEL), lambda i: (i, 0)),
        compiler_params=_params(("arbitrary",)),
        name="ret_out_proj",
    )(o, w_out, x2, post_g)


def _absmax(x):
    return jnp.max(jnp.max(jnp.abs(x), axis=0, keepdims=True), axis=1, keepdims=True)


def _pow2_scale(amax):
    e = jnp.floor(jnp.log2(jnp.maximum(amax, FP8_MIN_AMAX))) + (1.0 - math.log2(FP8_TARGET))
    return jnp.exp2(e)


def _hi_lo(x):
    hi = x.astype(FP8).astype(F32)
    lo = (x - hi).astype(FP8).astype(F32)
    return hi, lo


def _kv_proj_kernel(x_ref, g_ref, wk_ref, wvt_ref, cos_ref, sin_ref, k8_ref, sk_ref, vt_ref):
    h = _normed(x_ref, g_ref)
    cos = cos_ref[...]
    sin = sin_ref[...]
    kf = jnp.dot(h, wk_ref[...], preferred_element_type=F32)
    lane = lax.broadcasted_iota(jnp.int32, (PROJ_TM, LANES), 1)
    first_half = (lane & (DIFF_HEAD_DIM - 1)) < (DIFF_HEAD_DIM // 2)
    map1 = lane < DIFF_HEAD_DIM
    for s in range(DIFF_HEADS):
        xs = kf[:, s * LANES:(s + 1) * LANES]
        rot = jnp.where(first_half,
                        pltpu.roll(xs, LANES - DIFF_HEAD_DIM // 2, 1),
                        pltpu.roll(xs, DIFF_HEAD_DIM // 2, 1))
        kr = xs * cos + rot * sin
        sk = _pow2_scale(_absmax(kr))
        hi, lo = _hi_lo(kr * (1.0 / sk))
        lo_sw = pltpu.roll(lo, DIFF_HEAD_DIM, 1)
        c0 = s * ATT_K8_LANES
        k8_ref[:, c0:c0 + LANES] = jnp.where(map1, hi, lo_sw).astype(FP8)
        k8_ref[:, c0 + LANES:c0 + 2 * LANES] = jnp.where(map1, hi, 0.0).astype(FP8)
        k8_ref[:, c0 + 2 * LANES:c0 + 3 * LANES] = jnp.where(map1, lo_sw, hi).astype(FP8)
        k8_ref[:, c0 + 3 * LANES:c0 + 4 * LANES] = jnp.where(map1, 0.0, hi).astype(FP8)
        sk_ref[0, 0, s] = jnp.broadcast_to(sk, (SUBLANES, LANES))
    vt = lax.dot_general(wvt_ref[...], h, NT_DIMS, preferred_element_type=F32)
    ones_rows = jnp.ones((ATT_ONES_ROWS, PROJ_TM), BF16)
    for hd in range(DIFF_HEADS):
        r0 = hd * ATT_VT_ROWS
        vt_ref[0, r0:r0 + DIFF_V_DIM, :] = vt[hd * DIFF_V_DIM:(hd + 1) * DIFF_V_DIM, :].astype(BF16)
        vt_ref[0, r0 + DIFF_V_DIM:r0 + ATT_VT_ROWS, :] = ones_rows


def _q_proj_kernel(x_ref, g_ref, wt_ref, cos_ref, sin_ref, q8_ref, sq_ref, gt_ref):
    h = _normed(x_ref, g_ref)
    cos = cos_ref[...]
    sin = sin_ref[...]
    scale = DIFF_HEAD_DIM ** -0.5 * math.log2(math.e)
    dh = DIFF_HEAD_DIM
    half = dh // 2
    zeros = jnp.zeros((dh, PROJ_TM), FP8)
    heads_per_chunk = PROJ_ROWS // (2 * dh)
    for c in range((DIFF_QK_WIDTH + DIFF_WIDTH) // PROJ_ROWS):
        c0 = c * PROJ_ROWS
        pt = lax.dot_general(wt_ref[c0:c0 + PROJ_ROWS, :], h, NT_DIMS,
                             preferred_element_type=F32)
        if c0 < DIFF_QK_WIDTH:
            for hh in range(heads_per_chunk):
                head = c * heads_per_chunk + hh
                roped = []
                for mp in range(2):
                    r0 = (hh * 2 + mp) * dh
                    x1 = pt[r0:r0 + half, :]
                    x2 = pt[r0 + half:r0 + dh, :]
                    roped.append(((x1 * cos - x2 * sin) * scale, (x1 * sin + x2 * cos) * scale))
                amax = _absmax(roped[0][0])
                for part in (roped[0][1], roped[1][0], roped[1][1]):
                    amax = jnp.maximum(amax, _absmax(part))
                sq = _pow2_scale(amax)
                inv_sq = 1.0 / sq
                sq_ref[0, 0, head] = jnp.broadcast_to(sq, (SUBLANES, LANES))
                for mp in range(2):
                    base = (head * 2 + mp) * ATT_Q8_ROWS
                    split = [_hi_lo(part * inv_sq) for part in roped[mp]]
                    for blk, term in zip((0, 1, 2 + mp), (0, 0, 1)):
                        for hf in range(2):
                            r = base + blk * dh + hf * half
                            q8_ref[0, r:r + half, :] = split[hf][term].astype(FP8)
                    z0 = base + (3 - mp) * dh
                    q8_ref[0, z0:z0 + dh, :] = zeros
        else:
            g0 = c0 - DIFF_QK_WIDTH
            gt_ref[0, g0:g0 + PROJ_ROWS, :] = pt.astype(BF16)


def _diff_attn_kernel(q8_ref, sq_ref, k8_ref, sk_ref, vt_ref, gt_ref, sub_ref, bias_ref,
                      lq1_ref, lk1_ref, lq2_ref, lk2_ref, ot_ref, acc_ref, m_ref,
                      sa_ref, sb_ref, sc_ref, ma_ref, mb_ref, mc_ref, *, lambda_init, n_q_tiles):
    tq, tk = ATT_TQ, ATT_TK
    dv = DIFF_V_DIM
    maps = tuple((slice(mp * ATT_K8_LANES // 2, (mp + 1) * ATT_K8_LANES // 2),
                  slice(mp * ATT_Q8_ROWS, (mp + 1) * ATT_Q8_ROWS),
                  slice(mp * tq, (mp + 1) * tq)) for mp in range(2))
    buf_a, buf_b, buf_c = (sa_ref, ma_ref), (sb_ref, mb_ref), (sc_ref, mc_ref)
    lam = (jnp.exp(jnp.sum(lq1_ref[...] * lk1_ref[...], axis=-1, keepdims=True))
           - jnp.exp(jnp.sum(lq2_ref[...] * lk2_ref[...], axis=-1, keepdims=True))
           + lambda_init)

    def score_scale(j, qi):
        return sk_ref[0, j, 0][0:1, 0:1] * sq_ref[0, qi, 0][0:1, 0:1]

    def scores(buf, j, qi):
        s_ref, tile_max_ref = buf
        kk = pl.multiple_of(j * tk, tk)
        q0 = pl.multiple_of(qi * tq, tq)
        c = score_scale(j, qi)
        for k_lanes, q_rows, cols in maps:
            s = jnp.dot(k8_ref[0, pl.ds(kk, tk), k_lanes], q8_ref[0, q_rows, pl.ds(q0, tq)],
                        preferred_element_type=F32).astype(BF16)
            s_ref[:, cols] = s
            tile_max_ref[:, cols] = jnp.max(s, axis=0, keepdims=True).astype(F32) * c

    def consume(buf, j, qi, diagonal):
        s_ref, tile_max_ref = buf
        kk = pl.multiple_of(j * tk, tk)
        vt = vt_ref[0, :, pl.ds(kk, tk)]
        c = score_scale(j, qi)
        c16 = c.astype(BF16)
        for _, _, cols in maps:
            m_old = m_ref[:, cols]
            if diagonal:
                s = s_ref[:, cols] * c16 + bias_ref[...]
                m_new = jnp.maximum(m_old, jnp.max(s, axis=0, keepdims=True).astype(F32))
                p = jnp.exp2(s - m_new.astype(BF16))
            else:
                m_new = jnp.maximum(m_old, tile_max_ref[:, cols])
                p = jnp.exp2(s_ref[:, cols] * c16 - m_new.astype(BF16))
            alpha = jnp.exp2(m_old - m_new)
            pv = jnp.dot(vt, p, preferred_element_type=F32)
            acc_ref[:, cols] = alpha * acc_ref[:, cols] + pv
            m_ref[:, cols] = m_new

    def query_tile(qi, carry):
        m_ref[...] = jnp.full(m_ref.shape, MASK_NEG, F32)
        acc_ref[...] = jnp.zeros(acc_ref.shape, F32)

        def steady(i, carry):
            j = 3 * i
            scores(buf_b, j + 1, qi)
            consume(buf_a, j, qi, False)
            scores(buf_c, j + 2, qi)
            consume(buf_b, j + 1, qi, False)
            scores(buf_a, j + 3, qi)
            consume(buf_c, j + 2, qi, False)
            return carry

        trips = qi // 3
        lax.fori_loop(0, trips, steady, 0)
        j0 = 3 * trips
        left = qi + 1 - j0

        @pl.when(left == 1)
        def _():
            consume(buf_a, qi, qi, True)

        @pl.when(left == 2)
        def _():
            scores(buf_b, qi, qi)
            consume(buf_a, j0, qi, False)
            consume(buf_b, qi, qi, True)

        @pl.when(left == 3)
        def _():
            scores(buf_b, j0 + 1, qi)
            consume(buf_a, j0, qi, False)
            scores(buf_c, qi, qi)
            consume(buf_b, j0 + 1, qi, False)
            consume(buf_c, qi, qi, True)

        q0 = pl.multiple_of(qi * tq, tq)
        a = acc_ref[0:dv, :] * (1.0 / acc_ref[dv:dv + 1, :])
        scores(buf_a, 0, jnp.minimum(qi + 1, n_q_tiles - 1))
        o = a[:, 0:tq] - lam * a[:, tq:2 * tq]
        ms = jnp.mean(o * o, axis=0, keepdims=True)
        on = o * lax.rsqrt(ms + EPS) * sub_ref[...] * (1.0 - lambda_init)
        gate = gt_ref[0, :, pl.ds(q0, tq)].astype(F32)
        ot_ref[0, :, pl.ds(q0, tq)] = (on * _silu(gate)).astype(BF16)
        return carry

    scores(buf_a, 0, 0)
    lax.fori_loop(0, n_q_tiles, query_tile, 0)


def _out_proj_t_kernel(ot_ref, w_ref, x_ref, g_ref, xo_ref):
    g = g_ref[...]
    for c in range(PROJ_TM // OUT_ROWS):
        rows = slice(c * OUT_ROWS, (c + 1) * OUT_ROWS)
        y = lax.dot_general(ot_ref[0, :, rows], w_ref[...], TN_DIMS, preferred_element_type=F32)
        ms = jnp.mean(y * y, axis=-1, keepdims=True)
        xo_ref[rows, :] = x_ref[rows, :] + y * lax.rsqrt(ms + EPS) * g


def _kv_proj(x2, batch, seq, kv_g, wk, wvt, rope_tabs):
    tokens = x2.shape[0]
    cos, sin = rope_tabs
    nst = seq // PROJ_TM
    return pl.pallas_call(
        _kv_proj_kernel,
        out_shape=(jax.ShapeDtypeStruct((tokens, DIFF_HEADS * ATT_K8_LANES), FP8),
                   jax.ShapeDtypeStruct((batch, nst, DIFF_HEADS, SUBLANES, LANES), F32),
                   jax.ShapeDtypeStruct((batch, DIFF_HEADS * ATT_VT_ROWS, seq), BF16)),
        grid=(batch, nst),
        in_specs=[
            pl.BlockSpec((PROJ_TM, D_MODEL), lambda b, i: (b * nst + i, 0)),
            pl.BlockSpec((1, D_MODEL), lambda b, i: (0, 0)),
            pl.BlockSpec((D_MODEL, DIFF_QK_WIDTH), lambda b, i: (0, 0)),
            pl.BlockSpec((DIFF_WIDTH, D_MODEL), lambda b, i: (0, 0)),
            pl.BlockSpec((PROJ_TM, LANES), lambda b, i: (i, 0)),
            pl.BlockSpec((PROJ_TM, LANES), lambda b, i: (i, 0)),
        ],
        out_specs=(pl.BlockSpec((PROJ_TM, DIFF_HEADS * ATT_K8_LANES), lambda b, i: (b * nst + i, 0)),
                   pl.BlockSpec((1, 1, DIFF_HEADS, SUBLANES, LANES), lambda b, i: (b, i, 0, 0, 0)),
                   pl.BlockSpec((1, DIFF_HEADS * ATT_VT_ROWS, PROJ_TM), lambda b, i: (b, 0, i))),
        compiler_params=_params(("arbitrary", "arbitrary")),
        name="kv_proj",
    )(x2, kv_g, wk, wvt, cos, sin)


def _diff_layer(x2, batch, seq, layer, pre_g, post_g, wt_in, w_out, k8, sk, vt_sh, rope_t_tabs,
                diag_bias, sub_g, lq1, lk1, lq2, lk2):
    tokens = x2.shape[0]
    cos_t, sin_t = rope_t_tabs
    nst = seq // PROJ_TM
    q8, sq, gt = pl.pallas_call(
        _q_proj_kernel,
        out_shape=(jax.ShapeDtypeStruct((batch, DIFF_HEADS * 2 * ATT_Q8_ROWS, seq), FP8),
                   jax.ShapeDtypeStruct((batch, nst, DIFF_HEADS, SUBLANES, LANES), F32),
                   jax.ShapeDtypeStruct((batch, DIFF_WIDTH, seq), BF16)),
        grid=(batch, nst),
        in_specs=[
            pl.BlockSpec((PROJ_TM, D_MODEL), lambda b, i: (b * nst + i, 0)),
            pl.BlockSpec((1, D_MODEL), lambda b, i: (0, 0)),
            pl.BlockSpec((DIFF_QK_WIDTH + DIFF_WIDTH, D_MODEL), lambda b, i: (0, 0)),
            pl.BlockSpec((DIFF_HEAD_DIM // 2, PROJ_TM), lambda b, i: (0, i)),
            pl.BlockSpec((DIFF_HEAD_DIM // 2, PROJ_TM), lambda b, i: (0, i)),
        ],
        out_specs=(pl.BlockSpec((1, DIFF_HEADS * 2 * ATT_Q8_ROWS, PROJ_TM), lambda b, i: (b, 0, i)),
                   pl.BlockSpec((1, 1, DIFF_HEADS, SUBLANES, LANES), lambda b, i: (b, i, 0, 0, 0)),
                   pl.BlockSpec((1, DIFF_WIDTH, PROJ_TM), lambda b, i: (b, 0, i))),
        compiler_params=_params(("arbitrary", "arbitrary")),
        name="q_proj",
    )(x2, pre_g, wt_in, cos_t, sin_t)

    lambda_init = 0.8 - 0.6 * math.exp(-0.3 * layer)
    assert ATT_TQ == PROJ_TM and ATT_TK == PROJ_TM
    nq = seq // ATT_TQ
    k3 = k8.reshape(batch, seq, DIFF_HEADS * ATT_K8_LANES)
    scale_spec = pl.BlockSpec((1, nst, 1, SUBLANES, LANES), lambda b, h: (b, 0, h, 0, 0))
    lam_spec = pl.BlockSpec((1, DIFF_HEAD_DIM), lambda b, h: (0, 0))
    ot = pl.pallas_call(
        functools.partial(_diff_attn_kernel, lambda_init=lambda_init, n_q_tiles=nq),
        out_shape=jax.ShapeDtypeStruct((batch, DIFF_WIDTH, seq), BF16),
        grid=(batch, DIFF_HEADS),
        in_specs=[
            pl.BlockSpec((1, 2 * ATT_Q8_ROWS, seq), lambda b, h: (b, h, 0)),
            scale_spec,
            pl.BlockSpec((1, seq, ATT_K8_LANES), lambda b, h: (b, 0, h)),
            scale_spec,
            pl.BlockSpec((1, ATT_VT_ROWS, seq), lambda b, h: (b, h, 0)),
            pl.BlockSpec((1, DIFF_V_DIM, seq), lambda b, h: (b, h, 0)),
            pl.BlockSpec((DIFF_V_DIM, 1), lambda b, h: (0, 0)),
            pl.BlockSpec((ATT_TK, ATT_TQ), lambda b, h: (0, 0)),
            lam_spec, lam_spec, lam_spec, lam_spec,
        ],
        out_specs=pl.BlockSpec((1, DIFF_V_DIM, seq), lambda b, h: (b, h, 0)),
        scratch_shapes=[
            pltpu.VMEM((DIFF_V_DIM + ATT_ONES_ROWS, 2 * ATT_TQ), F32),
            pltpu.VMEM((1, 2 * ATT_TQ), F32),
            pltpu.VMEM((ATT_TK, 2 * ATT_TQ), BF16),
            pltpu.VMEM((ATT_TK, 2 * ATT_TQ), BF16),
            pltpu.VMEM((ATT_TK, 2 * ATT_TQ), BF16),
            pltpu.VMEM((1, 2 * ATT_TQ), F32),
            pltpu.VMEM((1, 2 * ATT_TQ), F32),
            pltpu.VMEM((1, 2 * ATT_TQ), F32),
        ],
        compiler_params=_params(("arbitrary", "arbitrary")),
        name="diff_attn",
    )(q8, sq, k3, sk, vt_sh, gt, sub_g, diag_bias, lq1, lk1, lq2, lk2)

    return pl.pallas_call(
        _out_proj_t_kernel,
        out_shape=jax.ShapeDtypeStruct((tokens, D_MODEL), F32),
        grid=(batch, nst),
        in_specs=[
            pl.BlockSpec((1, DIFF_WIDTH, PROJ_TM), lambda b, i: (b, 0, i)),
            pl.BlockSpec((DIFF_WIDTH, D_MODEL), lambda b, i: (0, 0)),
            pl.BlockSpec((PROJ_TM, D_MODEL), lambda b, i: (b * nst + i, 0)),
            pl.BlockSpec((1, D_MODEL), lambda b, i: (0, 0)),
        ],
        out_specs=pl.BlockSpec((PROJ_TM, D_MODEL), lambda b, i: (b * nst + i, 0)),
        compiler_params=_params(("arbitrary", "arbitrary")),
        name="diff_out_proj",
    )(ot, w_out, x2, post_g)


def kernel(x, pre_norm, post_norm, w_in_a, w_out_a, kv_norm, w_kv, w_in_b,
           lam_q1, lam_k1, lam_q2, lam_k2, sub_norm_b, w_out_b):
    batch, seq, _ = x.shape
    x2 = x.reshape(batch * seq, D_MODEL)

    ret_rope = _rope_tables_rows(seq, RET_QK_DIM)
    ret_tabs = _retention_tables()
    for layer in range(N_A_LAYERS):
        x2 = _retention_layer(
            x2, batch, seq,
            pre_norm[layer].reshape(1, D_MODEL), post_norm[layer].reshape(1, D_MODEL),
            w_in_a[layer].astype(BF16), w_out_a[layer].astype(BF16), ret_rope, ret_tabs)

    diff_rope = _rope_tables_rows(seq, DIFF_HEAD_DIM)
    k8, sk, vt_sh = _kv_proj(
        x2, batch, seq, kv_norm.reshape(1, D_MODEL),
        w_kv[:, :DIFF_QK_WIDTH].astype(BF16), w_kv[:, DIFF_QK_WIDTH:].T.astype(BF16), diff_rope)

    half = DIFF_HEAD_DIM // 2
    inv = ROPE_THETA ** (-jnp.arange(half, dtype=F32) / half)
    ang_t = inv[:, None] * jnp.arange(seq, dtype=F32)[None, :]
    rope_t = (jnp.cos(ang_t), jnp.sin(ang_t))
    tile_chunk = jnp.arange(ATT_TK, dtype=jnp.int32) // CHUNK
    diag_bias = jnp.where(tile_chunk[:, None] <= tile_chunk[None, :], 0.0, MASK_NEG).astype(BF16)
    for layer in range(N_A_LAYERS, DEPTH):
        j = layer - N_A_LAYERS
        x2 = _diff_layer(
            x2, batch, seq, layer,
            pre_norm[layer].reshape(1, D_MODEL), post_norm[layer].reshape(1, D_MODEL),
            w_in_b[j].T.astype(BF16), w_out_b[j].astype(BF16), k8, sk, vt_sh, rope_t, diag_bias,
            sub_norm_b[j].reshape(DIFF_V_DIM, 1),
            lam_q1[j].reshape(1, DIFF_HEAD_DIM), lam_k1[j].reshape(1, DIFF_HEAD_DIM),
            lam_q2[j].reshape(1, DIFF_HEAD_DIM), lam_k2[j].reshape(1, DIFF_HEAD_DIM))
    return x2.reshape(batch, seq, D_MODEL)
```

```python
import functools
import math

import jax
import jax.numpy as jnp
from jax import lax
from jax.experimental import pallas as pl
from jax.experimental.pallas import tpu as pltpu

F32 = jnp.float32
BF16 = jnp.bfloat16

D_MODEL = 1024
DEPTH = 4
N_A_LAYERS = DEPTH // 2
CHUNK = 64
EPS = 1e-6
ROPE_THETA = 10000.0

RET_HEADS = 8
RET_QK_DIM = 128
RET_V_DIM = 256
RET_QK_WIDTH = RET_HEADS * RET_QK_DIM
RET_WIDTH = RET_HEADS * RET_V_DIM
RET_IN_WIDTH = 2 * RET_QK_WIDTH + 2 * RET_WIDTH

DIFF_HEADS = 8
DIFF_HEAD_DIM = 64
DIFF_V_DIM = 128
DIFF_QK_WIDTH = DIFF_HEADS * 2 * DIFF_HEAD_DIM
DIFF_WIDTH = DIFF_HEADS * DIFF_V_DIM

LANES = 128
SUBLANES = 8
VMEM_LIMIT = 56 * 1024 * 1024

PROJ_TM = 512
PROJ_ROWS = 512
OUT_ROWS = 256
RET_BLOCK = 1024
RET_CHUNK = 256
RET_HEADS_PER_STEP = 4
ATT_TQ = 512
ATT_TK = 512
FP8 = jnp.float8_e4m3fn
FP8_TARGET = 256.0
FP8_MIN_AMAX = 2.0 ** -30
ATT_K8_LANES = 8 * DIFF_HEAD_DIM
ATT_Q8_ROWS = 4 * DIFF_HEAD_DIM
ATT_ONES_ROWS = 16
ATT_VT_ROWS = DIFF_V_DIM + ATT_ONES_ROWS
MASK_NEG = -(2.0 ** 100)

NT_DIMS = (((1,), (1,)), ((), ()))
TN_DIMS = (((0,), (0,)), ((), ()))


def _params(sem):
    return pltpu.CompilerParams(dimension_semantics=sem, vmem_limit_bytes=VMEM_LIMIT)


def _normed(x_ref, g_ref):
    x = x_ref[...]
    ms = jnp.mean(x * x, axis=-1, keepdims=True)
    return (x * lax.rsqrt(ms + EPS) * g_ref[...]).astype(BF16)


def _silu(g):
    return g / (1.0 + jnp.exp(-g))


def _ret_in_proj_kernel(x_ref, g_ref, w_ref, cos_ref, sin_ref, o_ref):
    h = _normed(x_ref, g_ref)
    cos = cos_ref[...]
    sin = sin_ref[...]
    k_scale = RET_QK_DIM ** -0.5
    for j in range(RET_IN_WIDTH // D_MODEL):
        cols = slice(j * D_MODEL, (j + 1) * D_MODEL)
        r = jnp.dot(h, w_ref[:, cols], preferred_element_type=F32)
        if j < 2:
            for s in range(RET_HEADS):
                xs = r[:, s * LANES:(s + 1) * LANES]
                y = xs * cos + pltpu.roll(xs, RET_QK_DIM // 2, 1) * sin
                if j == 1:
                    y = y * k_scale
                o_ref[:, j * D_MODEL + s * LANES:j * D_MODEL + (s + 1) * LANES] = y.astype(BF16)
        else:
            o_ref[:, cols] = r.astype(BF16)


def _retention_kernel(q_ref, k_ref, v_ref, gate_ref, dm_ref, qd_ref, kd_ref, o_ref, state_ref):
    @pl.when(pl.program_id(2) == 0)
    def _():
        state_ref[...] = jnp.zeros_like(state_ref)

    for i in range(RET_BLOCK // RET_CHUNK):
        rows = slice(i * RET_CHUNK, (i + 1) * RET_CHUNK)
        for hd in range(RET_HEADS_PER_STEP):
            qk_cols = slice(hd * RET_QK_DIM, (hd + 1) * RET_QK_DIM)
            v_cols = slice(hd * RET_V_DIM, (hd + 1) * RET_V_DIM)
            dm = dm_ref[hd]
            qd = qd_ref[hd]
            kd = kd_ref[hd]
            cdec = qd[RET_CHUNK - 1:RET_CHUNK, :]
            q = q_ref[rows, qk_cols]
            k = k_ref[rows, qk_cols]
            v = v_ref[rows, v_cols]
            s = lax.dot_general(q, k, NT_DIMS, preferred_element_type=F32) * dm
            st = state_ref[hd]
            o = jnp.dot(s.astype(BF16), v, preferred_element_type=F32)
            o = o + qd * jnp.dot(q, st.astype(BF16), preferred_element_type=F32)
            kdk = (k.astype(F32) * kd).astype(BF16)
            state_ref[hd] = st * cdec + lax.dot_general(kdk, v, TN_DIMS, preferred_element_type=F32)
            mu = jnp.mean(o, axis=-1, keepdims=True)
            d = o - mu
            var = jnp.mean(d * d, axis=-1, keepdims=True)
            on = d * lax.rsqrt(var + EPS)
            o_ref[rows, v_cols] = (on * _silu(gate_ref[rows, v_cols].astype(F32))).astype(BF16)


def _out_proj_kernel(o_ref, w_ref, x_ref, g_ref, xo_ref):
    g = g_ref[...]
    for c in range(PROJ_TM // OUT_ROWS):
        rows = slice(c * OUT_ROWS, (c + 1) * OUT_ROWS)
        y = jnp.dot(o_ref[rows, :], w_ref[...], preferred_element_type=F32)
        ms = jnp.mean(y * y, axis=-1, keepdims=True)
        xo_ref[rows, :] = x_ref[rows, :] + y * lax.rsqrt(ms + EPS) * g


def _retention_tables():
    h = jnp.arange(RET_HEADS, dtype=F32)
    log_gamma = jnp.log1p(-jnp.exp2(-5.0 - h))
    pos = jnp.arange(RET_CHUNK, dtype=F32)
    diff = pos[:, None] - pos[None, :]
    dm = jnp.where(diff[None] >= 0,
                   jnp.exp(jnp.maximum(diff, 0.0)[None] * log_gamma[:, None, None]), 0.0)
    qd = jnp.exp((pos[None, :] + 1.0) * log_gamma[:, None])
    kd = jnp.exp((RET_CHUNK - 1.0 - pos[None, :]) * log_gamma[:, None])
    qd = jnp.broadcast_to(qd[:, :, None], (RET_HEADS, RET_CHUNK, RET_V_DIM))
    kd = jnp.broadcast_to(kd[:, :, None], (RET_HEADS, RET_CHUNK, RET_QK_DIM))
    return dm, qd, kd


def _rope_tables_rows(seq, dim):
    half = dim // 2
    inv = ROPE_THETA ** (-jnp.arange(half, dtype=F32) / half)
    ang = jnp.arange(seq, dtype=F32)[:, None] * inv[None, :]
    cos = jnp.cos(ang)
    sin = jnp.sin(ang)
    reps = LANES // dim
    cos_full = jnp.tile(jnp.concatenate([cos, cos], axis=1), (1, reps))
    sin_signed = jnp.tile(jnp.concatenate([-sin, sin], axis=1), (1, reps))
    return cos_full, sin_signed


def _retention_layer(x2, batch, seq, pre_g, post_g, w_in, w_out, rope_tabs, ret_tabs):
    tokens = x2.shape[0]
    cos, sin = rope_tabs
    n_seq_tiles = seq // PROJ_TM
    proj = pl.pallas_call(
        _ret_in_proj_kernel,
        out_shape=jax.ShapeDtypeStruct((tokens, RET_IN_WIDTH), BF16),
        grid=(tokens // PROJ_TM,),
        in_specs=[
            pl.BlockSpec((PROJ_TM, D_MODEL), lambda i: (i, 0)),
            pl.BlockSpec((1, D_MODEL), lambda i: (0, 0)),
            pl.BlockSpec((D_MODEL, RET_IN_WIDTH), lambda i: (0, 0)),
            pl.BlockSpec((PROJ_TM, LANES), lambda i: (i % n_seq_tiles, 0)),
            pl.BlockSpec((PROJ_TM, LANES), lambda i: (i % n_seq_tiles, 0)),
        ],
        out_specs=pl.BlockSpec((PROJ_TM, RET_IN_WIDTH), lambda i: (i, 0)),
        compiler_params=_params(("arbitrary",)),
        name="ret_in_proj",
    )(x2, pre_g, w_in, cos, sin)

    dm, qd, kd = ret_tabs
    nblk = seq // RET_BLOCK
    hps = RET_HEADS_PER_STEP
    qk_w, v_w = hps * RET_QK_DIM, hps * RET_V_DIM
    k_off = RET_QK_WIDTH // qk_w
    v_off = 2 * RET_QK_WIDTH // v_w
    g_off = (2 * RET_QK_WIDTH + RET_WIDTH) // v_w
    o = pl.pallas_call(
        _retention_kernel,
        out_shape=jax.ShapeDtypeStruct((tokens, RET_WIDTH), BF16),
        grid=(batch, RET_HEADS // hps, nblk),
        in_specs=[
            pl.BlockSpec((RET_BLOCK, qk_w), lambda b, h, c: (b * nblk + c, h)),
            pl.BlockSpec((RET_BLOCK, qk_w), lambda b, h, c: (b * nblk + c, k_off + h)),
            pl.BlockSpec((RET_BLOCK, v_w), lambda b, h, c: (b * nblk + c, v_off + h)),
            pl.BlockSpec((RET_BLOCK, v_w), lambda b, h, c: (b * nblk + c, g_off + h)),
            pl.BlockSpec((hps, RET_CHUNK, RET_CHUNK), lambda b, h, c: (h, 0, 0)),
            pl.BlockSpec((hps, RET_CHUNK, RET_V_DIM), lambda b, h, c: (h, 0, 0)),
            pl.BlockSpec((hps, RET_CHUNK, RET_QK_DIM), lambda b, h, c: (h, 0, 0)),
        ],
        out_specs=pl.BlockSpec((RET_BLOCK, v_w), lambda b, h, c: (b * nblk + c, h)),
        scratch_shapes=[pltpu.VMEM((hps, RET_QK_DIM, RET_V_DIM), F32)],
        compiler_params=_params(("arbitrary", "arbitrary", "arbitrary")),
        name="retention",
    )(proj, proj, proj, proj, dm, qd, kd)

    return pl.pallas_call(
        _out_proj_kernel,
        out_shape=jax.ShapeDtypeStruct((tokens, D_MODEL), F32),
        grid=(tokens // PROJ_TM,),
        in_specs=[
            pl.BlockSpec((PROJ_TM, RET_WIDTH), lambda i: (i, 0)),
            pl.BlockSpec((RET_WIDTH, D_MODEL), lambda i: (0, 0)),
            pl.BlockSpec((PROJ_TM, D_MODEL), lambda i: (i, 0)),
            pl.BlockSpec((1, D_MODEL), lambda i: (0, 0)),
        ],
        out_specs=pl.BlockSpec((PROJ_TM, D_MODEL), lambda i: (i, 0)),
        compiler_params=_params(("arbitrary",)),
        name="ret_out_proj",
    )(o, w_out, x2, post_g)


def _absmax(x):
    return jnp.max(jnp.max(jnp.abs(x), axis=0, keepdims=True), axis=1, keepdims=True)


def _pow2_scale(amax):
    e = jnp.floor(jnp.log2(jnp.maximum(amax, FP8_MIN_AMAX))) + (1.0 - math.log2(FP8_TARGET))
    return jnp.exp2(e)


def _hi_lo(x):
    hi = x.astype(FP8).astype(F32)
    lo = (x - hi).astype(FP8).astype(F32)
    return hi, lo


def _kv_proj_kernel(x_ref, g_ref, wk_ref, wvt_ref, cos_ref, sin_ref, k8_ref, sk_ref, vt_ref):
    h = _normed(x_ref, g_ref)
    cos = cos_ref[...]
    sin = sin_ref[...]
    kf = jnp.dot(h, wk_ref[...], preferred_element_type=F32)
    lane = lax.broadcasted_iota(jnp.int32, (PROJ_TM, LANES), 1)
    first_half = (lane & (DIFF_HEAD_DIM - 1)) < (DIFF_HEAD_DIM // 2)
    map1 = lane < DIFF_HEAD_DIM
    for s in range(DIFF_HEADS):
        xs = kf[:, s * LANES:(s + 1) * LANES]
        rot = jnp.where(first_half,
                        pltpu.roll(xs, LANES - DIFF_HEAD_DIM // 2, 1),
                        pltpu.roll(xs, DIFF_HEAD_DIM // 2, 1))
        kr = xs * cos + rot * sin
        sk = _pow2_scale(_absmax(kr))
        hi, lo = _hi_lo(kr * (1.0 / sk))
        lo_sw = pltpu.roll(lo, DIFF_HEAD_DIM, 1)
        c0 = s * ATT_K8_LANES
        k8_ref[:, c0:c0 + LANES] = jnp.where(map1, hi, lo_sw).astype(FP8)
        k8_ref[:, c0 + LANES:c0 + 2 * LANES] = jnp.where(map1, hi, 0.0).astype(FP8)
        k8_ref[:, c0 + 2 * LANES:c0 + 3 * LANES] = jnp.where(map1, lo_sw, hi).astype(FP8)
        k8_ref[:, c0 + 3 * LANES:c0 + 4 * LANES] = jnp.where(map1, 0.0, hi).astype(FP8)
        sk_ref[0, 0, s] = jnp.broadcast_to(sk, (SUBLANES, LANES))
    vt = lax.dot_general(wvt_ref[...], h, NT_DIMS, preferred_element_type=F32)
    ones_rows = jnp.ones((ATT_ONES_ROWS, PROJ_TM), BF16)
    for hd in range(DIFF_HEADS):
        r0 = hd * ATT_VT_ROWS
        vt_ref[0, r0:r0 + DIFF_V_DIM, :] = vt[hd * DIFF_V_DIM:(hd + 1) * DIFF_V_DIM, :].astype(BF16)
        vt_ref[0, r0 + DIFF_V_DIM:r0 + ATT_VT_ROWS, :] = ones_rows


def _q_proj_kernel(x_ref, g_ref, wt_ref, cos_ref, sin_ref, q8_ref, sq_ref, gt_ref):
    h = _normed(x_ref, g_ref)
    cos = cos_ref[...]
    sin = sin_ref[...]
    scale = DIFF_HEAD_DIM ** -0.5 * math.log2(math.e)
    dh = DIFF_HEAD_DIM
    half = dh // 2
    zeros = jnp.zeros((dh, PROJ_TM), FP8)
    heads_per_chunk = PROJ_ROWS // (2 * dh)
    for c in range((DIFF_QK_WIDTH + DIFF_WIDTH) // PROJ_ROWS):
        c0 = c * PROJ_ROWS
        pt = lax.dot_general(wt_ref[c0:c0 + PROJ_ROWS, :], h, NT_DIMS,
                             preferred_element_type=F32)
        if c0 < DIFF_QK_WIDTH:
            for hh in range(heads_per_chunk):
                head = c * heads_per_chunk + hh
                roped = []
                for mp in range(2):
                    r0 = (hh * 2 + mp) * dh
                    x1 = pt[r0:r0 + half, :]
                    x2 = pt[r0 + half:r0 + dh, :]
                    roped.append(((x1 * cos - x2 * sin) * scale, (x1 * sin + x2 * cos) * scale))
                amax = _absmax(roped[0][0])
                for part in (roped[0][1], roped[1][0], roped[1][1]):
                    amax = jnp.maximum(amax, _absmax(part))
                sq = _pow2_scale(amax)
                inv_sq = 1.0 / sq
                sq_ref[0, 0, head] = jnp.broadcast_to(sq, (SUBLANES, LANES))
                for mp in range(2):
                    base = (head * 2 + mp) * ATT_Q8_ROWS
                    split = [_hi_lo(part * inv_sq) for part in roped[mp]]
                    for blk, term in zip((0, 1, 2 + mp), (0, 0, 1)):
                        for hf in range(2):
                            r = base + blk * dh + hf * half
                            q8_ref[0, r:r + half, :] = split[hf][term].astype(FP8)
                    z0 = base + (3 - mp) * dh
                    q8_ref[0, z0:z0 + dh, :] = zeros
        else:
            g0 = c0 - DIFF_QK_WIDTH
            gt_ref[0, g0:g0 + PROJ_ROWS, :] = pt.astype(BF16)


def _diff_attn_kernel(q8_ref, sq_ref, k8_ref, sk_ref, vt_ref, gt_ref, sub_ref, bias_ref,
                      lq1_ref, lk1_ref, lq2_ref, lk2_ref, ot_ref, acc_ref, m_ref,
                      sa_ref, sb_ref, sc_ref, ma_ref, mb_ref, mc_ref, *, lambda_init, n_q_tiles):
    tq, tk = ATT_TQ, ATT_TK
    dv = DIFF_V_DIM
    maps = tuple((slice(mp * ATT_K8_LANES // 2, (mp + 1) * ATT_K8_LANES // 2),
                  slice(mp * ATT_Q8_ROWS, (mp + 1) * ATT_Q8_ROWS),
                  slice(mp * tq, (mp + 1) * tq)) for mp in range(2))
    buf_a, buf_b, buf_c = (sa_ref, ma_ref), (sb_ref, mb_ref), (sc_ref, mc_ref)
    lam = (jnp.exp(jnp.sum(lq1_ref[...] * lk1_ref[...], axis=-1, keepdims=True))
           - jnp.exp(jnp.sum(lq2_ref[...] * lk2_ref[...], axis=-1, keepdims=True))
           + lambda_init)

    def score_scale(j, qi):
        return sk_ref[0, j, 0][0:1, 0:1] * sq_ref[0, qi, 0][0:1, 0:1]

    def scores(buf, j, qi):
        s_ref, tile_max_ref = buf
        kk = pl.multiple_of(j * tk, tk)
        q0 = pl.multiple_of(qi * tq, tq)
        c = score_scale(j, qi)
        for k_lanes, q_rows, cols in maps:
            s = jnp.dot(k8_ref[0, pl.ds(kk, tk), k_lanes], q8_ref[0, q_rows, pl.ds(q0, tq)],
                        preferred_element_type=F32).astype(BF16)
            s_ref[:, cols] = s
            tile_max_ref[:, cols] = jnp.max(s, axis=0, keepdims=True).astype(F32) * c

    def consume(buf, j, qi, diagonal):
        s_ref, tile_max_ref = buf
        kk = pl.multiple_of(j * tk, tk)
        vt = vt_ref[0, :, pl.ds(kk, tk)]
        c = score_scale(j, qi)
        c16 = c.astype(BF16)
        for _, _, cols in maps:
            m_old = m_ref[:, cols]
            if diagonal:
                s = s_ref[:, cols] * c16 + bias_ref[...]
                m_new = jnp.maximum(m_old, jnp.max(s, axis=0, keepdims=True).astype(F32))
                p = jnp.exp2(s - m_new.astype(BF16))
            else:
                m_new = jnp.maximum(m_old, tile_max_ref[:, cols])
                p = jnp.exp2(s_ref[:, cols] * c16 - m_new.astype(BF16))
            alpha = jnp.exp2(m_old - m_new)
            pv = jnp.dot(vt, p, preferred_element_type=F32)
            acc_ref[:, cols] = alpha * acc_ref[:, cols] + pv
            m_ref[:, cols] = m_new

    def query_tile(qi, carry):
        m_ref[...] = jnp.full(m_ref.shape, MASK_NEG, F32)
        acc_ref[...] = jnp.zeros(acc_ref.shape, F32)

        def steady(i, carry):
            j = 3 * i
            scores(buf_b, j + 1, qi)
            consume(buf_a, j, qi, False)
            scores(buf_c, j + 2, qi)
            consume(buf_b, j + 1, qi, False)
            scores(buf_a, j + 3, qi)
            consume(buf_c, j + 2, qi, False)
            return carry

        trips = qi // 3
        lax.fori_loop(0, trips, steady, 0)
        j0 = 3 * trips
        left = qi + 1 - j0

        @pl.when(left == 1)
        def _():
            consume(buf_a, qi, qi, True)

        @pl.when(left == 2)
        def _():
            scores(buf_b, qi, qi)
            consume(buf_a, j0, qi, False)
            consume(buf_b, qi, qi, True)

        @pl.when(left == 3)
        def _():
            scores(buf_b, j0 + 1, qi)
            consume(buf_a, j0, qi, False)
            scores(buf_c, qi, qi)
            consume(buf_b, j0 + 1, qi, False)
            consume(buf_c, qi, qi, True)

        q0 = pl.multiple_of(qi * tq, tq)
        a = acc_ref[0:dv, :] * (1.0 / acc_ref[dv:dv + 1, :])
        scores(buf_a, 0, jnp.minimum(qi + 1, n_q_tiles - 1))
        o = a[:, 0:tq] - lam * a[:, tq:2 * tq]
        ms = jnp.mean(o * o, axis=0, keepdims=True)
        on = o * lax.rsqrt(ms + EPS) * sub_ref[...] * (1.0 - lambda_init)
        gate = gt_ref[0, :, pl.ds(q0, tq)].astype(F32)
        ot_ref[0, :, pl.ds(q0, tq)] = (on * _silu(gate)).astype(BF16)
        return carry

    scores(buf_a, 0, 0)
    lax.fori_loop(0, n_q_tiles, query_tile, 0)


def _out_proj_t_kernel(ot_ref, w_ref, x_ref, g_ref, xo_ref):
    g = g_ref[...]
    for c in range(PROJ_TM // OUT_ROWS):
        rows = slice(c * OUT_ROWS, (c + 1) * OUT_ROWS)
        y = lax.dot_general(ot_ref[0, :, rows], w_ref[...], TN_DIMS, preferred_element_type=F32)
        ms = jnp.mean(y * y, axis=-1, keepdims=True)
        xo_ref[rows, :] = x_ref[rows, :] + y * lax.rsqrt(ms + EPS) * g


def _kv_proj(x2, batch, seq, kv_g, wk, wvt, rope_tabs):
    tokens = x2.shape[0]
    cos, sin = rope_tabs
    nst = seq // PROJ_TM
    return pl.pallas_call(
        _kv_proj_kernel,
        out_shape=(jax.ShapeDtypeStruct((tokens, DIFF_HEADS * ATT_K8_LANES), FP8),
                   jax.ShapeDtypeStruct((batch, nst, DIFF_HEADS, SUBLANES, LANES), F32),
                   jax.ShapeDtypeStruct((batch, DIFF_HEADS * ATT_VT_ROWS, seq), BF16)),
        grid=(batch, nst),
        in_specs=[
            pl.BlockSpec((PROJ_TM, D_MODEL), lambda b, i: (b * nst + i, 0)),
            pl.BlockSpec((1, D_MODEL), lambda b, i: (0, 0)),
            pl.BlockSpec((D_MODEL, DIFF_QK_WIDTH), lambda b, i: (0, 0)),
            pl.BlockSpec((DIFF_WIDTH, D_MODEL), lambda b, i: (0, 0)),
            pl.BlockSpec((PROJ_TM, LANES), lambda b, i: (i, 0)),
            pl.BlockSpec((PROJ_TM, LANES), lambda b, i: (i, 0)),
        ],
        out_specs=(pl.BlockSpec((PROJ_TM, DIFF_HEADS * ATT_K8_LANES), lambda b, i: (b * nst + i, 0)),
                   pl.BlockSpec((1, 1, DIFF_HEADS, SUBLANES, LANES), lambda b, i: (b, i, 0, 0, 0)),
                   pl.BlockSpec((1, DIFF_HEADS * ATT_VT_ROWS, PROJ_TM), lambda b, i: (b, 0, i))),
        compiler_params=_params(("arbitrary", "arbitrary")),
        name="kv_proj",
    )(x2, kv_g, wk, wvt, cos, sin)


def _diff_layer(x2, batch, seq, layer, pre_g, post_g, wt_in, w_out, k8, sk, vt_sh, rope_t_tabs,
                diag_bias, sub_g, lq1, lk1, lq2, lk2):
    tokens = x2.shape[0]
    cos_t, sin_t = rope_t_tabs
    nst = seq // PROJ_TM
    q8, sq, gt = pl.pallas_call(
        _q_proj_kernel,
        out_shape=(jax.ShapeDtypeStruct((batch, DIFF_HEADS * 2 * ATT_Q8_ROWS, seq), FP8),
                   jax.ShapeDtypeStruct((batch, nst, DIFF_HEADS, SUBLANES, LANES), F32),
                   jax.ShapeDtypeStruct((batch, DIFF_WIDTH, seq), BF16)),
        grid=(batch, nst),
        in_specs=[
            pl.BlockSpec((PROJ_TM, D_MODEL), lambda b, i: (b * nst + i, 0)),
            pl.BlockSpec((1, D_MODEL), lambda b, i: (0, 0)),
            pl.BlockSpec((DIFF_QK_WIDTH + DIFF_WIDTH, D_MODEL), lambda b, i: (0, 0)),
            pl.BlockSpec((DIFF_HEAD_DIM // 2, PROJ_TM), lambda b, i: (0, i)),
            pl.BlockSpec((DIFF_HEAD_DIM // 2, PROJ_TM), lambda b, i: (0, i)),
        ],
        out_specs=(pl.BlockSpec((1, DIFF_HEADS * 2 * ATT_Q8_ROWS, PROJ_TM), lambda b, i: (b, 0, i)),
                   pl.BlockSpec((1, 1, DIFF_HEADS, SUBLANES, LANES), lambda b, i: (b, i, 0, 0, 0)),
                   pl.BlockSpec((1, DIFF_WIDTH, PROJ_TM), lambda b, i: (b, 0, i))),
        compiler_params=_params(("arbitrary", "arbitrary")),
        name="q_proj",
    )(x2, pre_g, wt_in, cos_t, sin_t)

    lambda_init = 0.8 - 0.6 * math.exp(-0.3 * layer)
    assert ATT_TQ == PROJ_TM and ATT_TK == PROJ_TM
    nq = seq // ATT_TQ
    k3 = k8.reshape(batch, seq, DIFF_HEADS * ATT_K8_LANES)
    scale_spec = pl.BlockSpec((1, nst, 1, SUBLANES, LANES), lambda b, h: (b, 0, h, 0, 0))
    lam_spec = pl.BlockSpec((1, DIFF_HEAD_DIM), lambda b, h: (0, 0))
    ot = pl.pallas_call(
        functools.partial(_diff_attn_kernel, lambda_init=lambda_init, n_q_tiles=nq),
        out_shape=jax.ShapeDtypeStruct((batch, DIFF_WIDTH, seq), BF16),
        grid=(batch, DIFF_HEADS),
        in_specs=[
            pl.BlockSpec((1, 2 * ATT_Q8_ROWS, seq), lambda b, h: (b, h, 0)),
            scale_spec,
            pl.BlockSpec((1, seq, ATT_K8_LANES), lambda b, h: (b, 0, h)),
            scale_spec,
            pl.BlockSpec((1, ATT_VT_ROWS, seq), lambda b, h: (b, h, 0)),
            pl.BlockSpec((1, DIFF_V_DIM, seq), lambda b, h: (b, h, 0)),
            pl.BlockSpec((DIFF_V_DIM, 1), lambda b, h: (0, 0)),
            pl.BlockSpec((ATT_TK, ATT_TQ), lambda b, h: (0, 0)),
            lam_spec, lam_spec, lam_spec, lam_spec,
        ],
        out_specs=pl.BlockSpec((1, DIFF_V_DIM, seq), lambda b, h: (b, h, 0)),
        scratch_shapes=[
            pltpu.VMEM((DIFF_V_DIM + ATT_ONES_ROWS, 2 * ATT_TQ), F32),
            pltpu.VMEM((1, 2 * ATT_TQ), F32),
            pltpu.VMEM((ATT_TK, 2 * ATT_TQ), BF16),
            pltpu.VMEM((ATT_TK, 2 * ATT_TQ), BF16),
            pltpu.VMEM((ATT_TK, 2 * ATT_TQ), BF16),
            pltpu.VMEM((1, 2 * ATT_TQ), F32),
            pltpu.VMEM((1, 2 * ATT_TQ), F32),
            pltpu.VMEM((1, 2 * ATT_TQ), F32),
        ],
        compiler_params=_params(("arbitrary", "arbitrary")),
        name="diff_attn",
    )(q8, sq, k3, sk, vt_sh, gt, sub_g, diag_bias, lq1, lk1, lq2, lk2)

    return pl.pallas_call(
        _out_proj_t_kernel,
        out_shape=jax.ShapeDtypeStruct((tokens, D_MODEL), F32),
        grid=(batch, nst),
        in_specs=[
            pl.BlockSpec((1, DIFF_WIDTH, PROJ_TM), lambda b, i: (b, 0, i)),
            pl.BlockSpec((DIFF_WIDTH, D_MODEL), lambda b, i: (0, 0)),
            pl.BlockSpec((PROJ_TM, D_MODEL), lambda b, i: (b * nst + i, 0)),
            pl.BlockSpec((1, D_MODEL), lambda b, i: (0, 0)),
        ],
        out_specs=pl.BlockSpec((PROJ_TM, D_MODEL), lambda b, i: (b * nst + i, 0)),
        compiler_params=_params(("arbitrary", "arbitrary")),
        name="diff_out_proj",
    )(ot, w_out, x2, post_g)


def kernel(x, pre_norm, post_norm, w_in_a, w_out_a, kv_norm, w_kv, w_in_b,
           lam_q1, lam_k1, lam_q2, lam_k2, sub_norm_b, w_out_b):
    batch, seq, _ = x.shape
    x2 = x.reshape(batch * seq, D_MODEL)

    ret_rope = _rope_tables_rows(seq, RET_QK_DIM)
    ret_tabs = _retention_tables()
    for layer in range(N_A_LAYERS):
        x2 = _retention_layer(
            x2, batch, seq,
            pre_norm[layer].reshape(1, D_MODEL), post_norm[layer].reshape(1, D_MODEL),
            w_in_a[layer].astype(BF16), w_out_a[layer].astype(BF16), ret_rope, ret_tabs)

    diff_rope = _rope_tables_rows(seq, DIFF_HEAD_DIM)
    k8, sk, vt_sh = _kv_proj(
        x2, batch, seq, kv_norm.reshape(1, D_MODEL),
        w_kv[:, :DIFF_QK_WIDTH].astype(BF16), w_kv[:, DIFF_QK_WIDTH:].T.astype(BF16), diff_rope)

    half = DIFF_HEAD_DIM // 2
    inv = ROPE_THETA ** (-jnp.arange(half, dtype=F32) / half)
    ang_t = inv[:, None] * jnp.arange(seq, dtype=F32)[None, :]
    rope_t = (jnp.cos(ang_t), jnp.sin(ang_t))
    tile_chunk = jnp.arange(ATT_TK, dtype=jnp.int32) // CHUNK
    diag_bias = jnp.where(tile_chunk[:, None] <= tile_chunk[None, :], 0.0, MASK_NEG).astype(BF16)
    for layer in range(N_A_LAYERS, DEPTH):
        j = layer - N_A_LAYERS
        x2 = _diff_layer(
            x2, batch, seq, layer,
            pre_norm[layer].reshape(1, D_MODEL), post_norm[layer].reshape(1, D_MODEL),
            w_in_b[j].T.astype(BF16), w_out_b[j].astype(BF16), k8, sk, vt_sh, rope_t, diag_bias,
            sub_norm_b[j].reshape(DIFF_V_DIM, 1),
            lam_q1[j].reshape(1, DIFF_HEAD_DIM), lam_k1[j].reshape(1, DIFF_HEAD_DIM),
            lam_q2[j].reshape(1, DIFF_HEAD_DIM), lam_k2[j].reshape(1, DIFF_HEAD_DIM))
    return x2.reshape(batch, seq, D_MODEL)
```

```python
import functools
import math

import jax
import jax.numpy as jnp
from jax import lax
from jax.experimental import pallas as pl
from jax.experimental.pallas import tpu as pltpu

F32 = jnp.float32
BF16 = jnp.bfloat16

D_MODEL = 1024
DEPTH = 4
N_A_LAYERS = DEPTH // 2
CHUNK = 64
EPS = 1e-6
ROPE_THETA = 10000.0

RET_HEADS = 8
RET_QK_DIM = 128
RET_V_DIM = 256
RET_QK_WIDTH = RET_HEADS * RET_QK_DIM
RET_WIDTH = RET_HEADS * RET_V_DIM
RET_IN_WIDTH = 2 * RET_QK_WIDTH + 2 * RET_WIDTH

DIFF_HEADS = 8
DIFF_HEAD_DIM = 64
DIFF_V_DIM = 128
DIFF_QK_WIDTH = DIFF_HEADS * 2 * DIFF_HEAD_DIM
DIFF_WIDTH = DIFF_HEADS * DIFF_V_DIM

LANES = 128
SUBLANES = 8
VMEM_LIMIT = 56 * 1024 * 1024

PROJ_TM = 512
PROJ_ROWS = 512
OUT_ROWS = 256
RET_BLOCK = 512
RET_CHUNK = 256
ATT_TQ = 512
ATT_TK = 512
FP8 = jnp.float8_e4m3fn
FP8_TARGET = 256.0
FP8_MIN_AMAX = 2.0 ** -30
ATT_K8_LANES = 8 * DIFF_HEAD_DIM
ATT_Q8_ROWS = 4 * DIFF_HEAD_DIM
ATT_ONES_ROWS = 16
ATT_VT_ROWS = DIFF_V_DIM + ATT_ONES_ROWS
MASK_NEG = -(2.0 ** 100)

NT_DIMS = (((1,), (1,)), ((), ()))
TN_DIMS = (((0,), (0,)), ((), ()))


def _params(sem):
    return pltpu.CompilerParams(dimension_semantics=sem, vmem_limit_bytes=VMEM_LIMIT)


def _normed(x_ref, g_ref):
    x = x_ref[...]
    ms = jnp.mean(x * x, axis=-1, keepdims=True)
    return (x * lax.rsqrt(ms + EPS) * g_ref[...]).astype(BF16)


def _silu(g):
    return g / (1.0 + jnp.exp(-g))


def _ret_in_proj_kernel(x_ref, g_ref, w_ref, cos_ref, sin_ref, o_ref):
    h = _normed(x_ref, g_ref)
    cos = cos_ref[...]
    sin = sin_ref[...]
    k_scale = RET_QK_DIM ** -0.5
    for j in range(RET_IN_WIDTH // D_MODEL):
        cols = slice(j * D_MODEL, (j + 1) * D_MODEL)
        r = jnp.dot(h, w_ref[:, cols], preferred_element_type=F32)
        if j < 2:
            for s in range(RET_HEADS):
                xs = r[:, s * LANES:(s + 1) * LANES]
                y = xs * cos + pltpu.roll(xs, RET_QK_DIM // 2, 1) * sin
                if j == 1:
                    y = y * k_scale
                o_ref[:, j * D_MODEL + s * LANES:j * D_MODEL + (s + 1) * LANES] = y.astype(BF16)
        else:
            o_ref[:, cols] = r.astype(BF16)


def _retention_kernel(q_ref, k_ref, v_ref, gate_ref, dm_ref, qd_ref, kd_ref, w_ref, x_ref, g_ref,
                      xo_ref, state_ref, o_ref):
    @pl.when(pl.program_id(1) == 0)
    def _():
        state_ref[...] = jnp.zeros_like(state_ref)

    post_g = g_ref[...]
    for i in range(RET_BLOCK // RET_CHUNK):
        rows = slice(i * RET_CHUNK, (i + 1) * RET_CHUNK)
        for hd in range(RET_HEADS):
            qk_cols = slice(hd * RET_QK_DIM, (hd + 1) * RET_QK_DIM)
            v_cols = slice(hd * RET_V_DIM, (hd + 1) * RET_V_DIM)
            dm = dm_ref[hd]
            qd = qd_ref[hd]
            kd = kd_ref[hd]
            cdec = qd[RET_CHUNK - 1:RET_CHUNK, :]
            q = q_ref[rows, qk_cols]
            k = k_ref[rows, qk_cols]
            v = v_ref[rows, v_cols]
            s = lax.dot_general(q, k, NT_DIMS, preferred_element_type=F32) * dm
            st = state_ref[hd]
            o = jnp.dot(s.astype(BF16), v, preferred_element_type=F32)
            o = o + qd * jnp.dot(q, st.astype(BF16), preferred_element_type=F32)
            kdk = (k.astype(F32) * kd).astype(BF16)
            state_ref[hd] = st * cdec + lax.dot_general(kdk, v, TN_DIMS, preferred_element_type=F32)
            mu = jnp.mean(o, axis=-1, keepdims=True)
            d = o - mu
            var = jnp.mean(d * d, axis=-1, keepdims=True)
            on = d * lax.rsqrt(var + EPS)
            o_ref[rows, v_cols] = (on * _silu(gate_ref[rows, v_cols].astype(F32))).astype(BF16)
        y = jnp.dot(o_ref[rows, :], w_ref[...], preferred_element_type=F32)
        ms = jnp.mean(y * y, axis=-1, keepdims=True)
        xo_ref[rows, :] = x_ref[rows, :] + y * lax.rsqrt(ms + EPS) * post_g


def _retention_tables():
    h = jnp.arange(RET_HEADS, dtype=F32)
    log_gamma = jnp.log1p(-jnp.exp2(-5.0 - h))
    pos = jnp.arange(RET_CHUNK, dtype=F32)
    diff = pos[:, None] - pos[None, :]
    dm = jnp.where(diff[None] >= 0,
                   jnp.exp(jnp.maximum(diff, 0.0)[None] * log_gamma[:, None, None]), 0.0)
    qd = jnp.exp((pos[None, :] + 1.0) * log_gamma[:, None])
    kd = jnp.exp((RET_CHUNK - 1.0 - pos[None, :]) * log_gamma[:, None])
    qd = jnp.broadcast_to(qd[:, :, None], (RET_HEADS, RET_CHUNK, RET_V_DIM))
    kd = jnp.broadcast_to(kd[:, :, None], (RET_HEADS, RET_CHUNK, RET_QK_DIM))
    return dm, qd, kd


def _rope_tables_rows(seq, dim):
    half = dim // 2
    inv = ROPE_THETA ** (-jnp.arange(half, dtype=F32) / half)
    ang = jnp.arange(seq, dtype=F32)[:, None] * inv[None, :]
    cos = jnp.cos(ang)
    sin = jnp.sin(ang)
    reps = LANES // dim
    cos_full = jnp.tile(jnp.concatenate([cos, cos], axis=1), (1, reps))
    sin_signed = jnp.tile(jnp.concatenate([-sin, sin], axis=1), (1, reps))
    return cos_full, sin_signed


def _retention_layer(x2, batch, seq, pre_g, post_g, w_in, w_out, rope_tabs, ret_tabs):
    tokens = x2.shape[0]
    cos, sin = rope_tabs
    n_seq_tiles = seq // PROJ_TM
    proj = pl.pallas_call(
        _ret_in_proj_kernel,
        out_shape=jax.ShapeDtypeStruct((tokens, RET_IN_WIDTH), BF16),
        grid=(tokens // PROJ_TM,),
        in_specs=[
            pl.BlockSpec((PROJ_TM, D_MODEL), lambda i: (i, 0)),
            pl.BlockSpec((1, D_MODEL), lambda i: (0, 0)),
            pl.BlockSpec((D_MODEL, RET_IN_WIDTH), lambda i: (0, 0)),
            pl.BlockSpec((PROJ_TM, LANES), lambda i: (i % n_seq_tiles, 0)),
            pl.BlockSpec((PROJ_TM, LANES), lambda i: (i % n_seq_tiles, 0)),
        ],
        out_specs=pl.BlockSpec((PROJ_TM, RET_IN_WIDTH), lambda i: (i, 0)),
        compiler_params=_params(("arbitrary",)),
        name="ret_in_proj",
    )(x2, pre_g, w_in, cos, sin)

    dm, qd, kd = ret_tabs
    nblk = seq // RET_BLOCK
    k_blk = 1
    v_blk = 2 * RET_QK_WIDTH // RET_WIDTH
    g_blk = v_blk + 1
    const3 = lambda b, c: (0, 0, 0)
    return pl.pallas_call(
        _retention_kernel,
        out_shape=jax.ShapeDtypeStruct((tokens, D_MODEL), F32),
        grid=(batch, nblk),
        in_specs=[
            pl.BlockSpec((RET_BLOCK, RET_QK_WIDTH), lambda b, c: (b * nblk + c, 0)),
            pl.BlockSpec((RET_BLOCK, RET_QK_WIDTH), lambda b, c: (b * nblk + c, k_blk)),
            pl.BlockSpec((RET_BLOCK, RET_WIDTH), lambda b, c: (b * nblk + c, v_blk)),
            pl.BlockSpec((RET_BLOCK, RET_WIDTH), lambda b, c: (b * nblk + c, g_blk)),
            pl.BlockSpec((RET_HEADS, RET_CHUNK, RET_CHUNK), const3),
            pl.BlockSpec((RET_HEADS, RET_CHUNK, RET_V_DIM), const3),
            pl.BlockSpec((RET_HEADS, RET_CHUNK, RET_QK_DIM), const3),
            pl.BlockSpec((RET_WIDTH, D_MODEL), lambda b, c: (0, 0)),
            pl.BlockSpec((RET_BLOCK, D_MODEL), lambda b, c: (b * nblk + c, 0)),
            pl.BlockSpec((1, D_MODEL), lambda b, c: (0, 0)),
        ],
        out_specs=pl.BlockSpec((RET_BLOCK, D_MODEL), lambda b, c: (b * nblk + c, 0)),
        scratch_shapes=[pltpu.VMEM((RET_HEADS, RET_QK_DIM, RET_V_DIM), F32),
                        pltpu.VMEM((RET_BLOCK, RET_WIDTH), BF16)],
        compiler_params=_params(("arbitrary", "arbitrary")),
        name="retention",
    )(proj, proj, proj, proj, dm, qd, kd, w_out, x2, post_g)


def _absmax(x):
    return jnp.max(jnp.max(jnp.abs(x), axis=0, keepdims=True), axis=1, keepdims=True)


def _pow2_scale(amax):
    e = jnp.floor(jnp.log2(jnp.maximum(amax, FP8_MIN_AMAX))) + (1.0 - math.log2(FP8_TARGET))
    return jnp.exp2(e)


def _hi_lo(x):
    hi = x.astype(FP8).astype(F32)
    lo = (x - hi).astype(FP8).astype(F32)
    return hi, lo


def _kv_proj_kernel(x_ref, g_ref, wk_ref, wvt_ref, cos_ref, sin_ref, k8_ref, sk_ref, vt_ref):
    h = _normed(x_ref, g_ref)
    cos = cos_ref[...]
    sin = sin_ref[...]
    kf = jnp.dot(h, wk_ref[...], preferred_element_type=F32)
    lane = lax.broadcasted_iota(jnp.int32, (PROJ_TM, LANES), 1)
    first_half = (lane & (DIFF_HEAD_DIM - 1)) < (DIFF_HEAD_DIM // 2)
    map1 = lane < DIFF_HEAD_DIM
    for s in range(DIFF_HEADS):
        xs = kf[:, s * LANES:(s + 1) * LANES]
        rot = jnp.where(first_half,
                        pltpu.roll(xs, LANES - DIFF_HEAD_DIM // 2, 1),
                        pltpu.roll(xs, DIFF_HEAD_DIM // 2, 1))
        kr = xs * cos + rot * sin
        sk = _pow2_scale(_absmax(kr))
        hi, lo = _hi_lo(kr * (1.0 / sk))
        lo_sw = pltpu.roll(lo, DIFF_HEAD_DIM, 1)
        c0 = s * ATT_K8_LANES
        k8_ref[:, c0:c0 + LANES] = jnp.where(map1, hi, lo_sw).astype(FP8)
        k8_ref[:, c0 + LANES:c0 + 2 * LANES] = jnp.where(map1, hi, 0.0).astype(FP8)
        k8_ref[:, c0 + 2 * LANES:c0 + 3 * LANES] = jnp.where(map1, lo_sw, hi).astype(FP8)
        k8_ref[:, c0 + 3 * LANES:c0 + 4 * LANES] = jnp.where(map1, 0.0, hi).astype(FP8)
        sk_ref[0, 0, s] = jnp.broadcast_to(sk, (SUBLANES, LANES))
    vt = lax.dot_general(wvt_ref[...], h, NT_DIMS, preferred_element_type=F32)
    ones_rows = jnp.ones((ATT_ONES_ROWS, PROJ_TM), BF16)
    for hd in range(DIFF_HEADS):
        r0 = hd * ATT_VT_ROWS
        vt_ref[0, r0:r0 + DIFF_V_DIM, :] = vt[hd * DIFF_V_DIM:(hd + 1) * DIFF_V_DIM, :].astype(BF16)
        vt_ref[0, r0 + DIFF_V_DIM:r0 + ATT_VT_ROWS, :] = ones_rows


def _q_proj_kernel(x_ref, g_ref, wt_ref, cos_ref, sin_ref, q8_ref, sq_ref, gt_ref):
    h = _normed(x_ref, g_ref)
    cos = cos_ref[...]
    sin = sin_ref[...]
    scale = DIFF_HEAD_DIM ** -0.5 * math.log2(math.e)
    dh = DIFF_HEAD_DIM
    half = dh // 2
    zeros = jnp.zeros((dh, PROJ_TM), FP8)
    heads_per_chunk = PROJ_ROWS // (2 * dh)
    for c in range((DIFF_QK_WIDTH + DIFF_WIDTH) // PROJ_ROWS):
        c0 = c * PROJ_ROWS
        pt = lax.dot_general(wt_ref[c0:c0 + PROJ_ROWS, :], h, NT_DIMS,
                             preferred_element_type=F32)
        if c0 < DIFF_QK_WIDTH:
            for hh in range(heads_per_chunk):
                head = c * heads_per_chunk + hh
                roped = []
                for mp in range(2):
                    r0 = (hh * 2 + mp) * dh
                    x1 = pt[r0:r0 + half, :]
                    x2 = pt[r0 + half:r0 + dh, :]
                    roped.append(((x1 * cos - x2 * sin) * scale, (x1 * sin + x2 * cos) * scale))
                amax = _absmax(roped[0][0])
                for part in (roped[0][1], roped[1][0], roped[1][1]):
                    amax = jnp.maximum(amax, _absmax(part))
                sq = _pow2_scale(amax)
                inv_sq = 1.0 / sq
                sq_ref[0, 0, head] = jnp.broadcast_to(sq, (SUBLANES, LANES))
                for mp in range(2):
                    base = (head * 2 + mp) * ATT_Q8_ROWS
                    split = [_hi_lo(part * inv_sq) for part in roped[mp]]
                    for blk, term in zip((0, 1, 2 + mp), (0, 0, 1)):
                        for hf in range(2):
                            r = base + blk * dh + hf * half
                            q8_ref[0, r:r + half, :] = split[hf][term].astype(FP8)
                    z0 = base + (3 - mp) * dh
                    q8_ref[0, z0:z0 + dh, :] = zeros
        else:
            g0 = c0 - DIFF_QK_WIDTH
            gt_ref[0, g0:g0 + PROJ_ROWS, :] = pt.astype(BF16)


def _diff_attn_kernel(q8_ref, sq_ref, k8_ref, sk_ref, vt_ref, gt_ref, sub_ref, bias_ref,
                      lq1_ref, lk1_ref, lq2_ref, lk2_ref, ot_ref, acc_ref, m_ref,
                      sa_ref, sb_ref, sc_ref, ma_ref, mb_ref, mc_ref, *, lambda_init, n_q_tiles):
    tq, tk = ATT_TQ, ATT_TK
    dv = DIFF_V_DIM
    maps = tuple((slice(mp * ATT_K8_LANES // 2, (mp + 1) * ATT_K8_LANES // 2),
                  slice(mp * ATT_Q8_ROWS, (mp + 1) * ATT_Q8_ROWS),
                  slice(mp * tq, (mp + 1) * tq)) for mp in range(2))
    buf_a, buf_b, buf_c = (sa_ref, ma_ref), (sb_ref, mb_ref), (sc_ref, mc_ref)
    lam = (jnp.exp(jnp.sum(lq1_ref[...] * lk1_ref[...], axis=-1, keepdims=True))
           - jnp.exp(jnp.sum(lq2_ref[...] * lk2_ref[...], axis=-1, keepdims=True))
           + lambda_init)

    def score_scale(j, qi):
        return sk_ref[0, j, 0][0:1, 0:1] * sq_ref[0, qi, 0][0:1, 0:1]

    def scores(buf, j, qi):
        s_ref, tile_max_ref = buf
        kk = pl.multiple_of(j * tk, tk)
        q0 = pl.multiple_of(qi * tq, tq)
        c = score_scale(j, qi)
        for k_lanes, q_rows, cols in maps:
            s = jnp.dot(k8_ref[0, pl.ds(kk, tk), k_lanes], q8_ref[0, q_rows, pl.ds(q0, tq)],
                        preferred_element_type=F32).astype(BF16)
            s_ref[:, cols] = s
            tile_max_ref[:, cols] = jnp.max(s, axis=0, keepdims=True).astype(F32) * c

    def consume(buf, j, qi, diagonal):
        s_ref, tile_max_ref = buf
        kk = pl.multiple_of(j * tk, tk)
        vt = vt_ref[0, :, pl.ds(kk, tk)]
        c = score_scale(j, qi)
        c16 = c.astype(BF16)
        for _, _, cols in maps:
            m_old = m_ref[:, cols]
            if diagonal:
                s = s_ref[:, cols] * c16 + bias_ref[...]
                m_new = jnp.maximum(m_old, jnp.max(s, axis=0, keepdims=True).astype(F32))
                p = jnp.exp2(s - m_new.astype(BF16))
            else:
                m_new = jnp.maximum(m_old, tile_max_ref[:, cols])
                p = jnp.exp2(s_ref[:, cols] * c16 - m_new.astype(BF16))
            alpha = jnp.exp2(m_old - m_new)
            pv = jnp.dot(vt, p, preferred_element_type=F32)
            acc_ref[:, cols] = alpha * acc_ref[:, cols] + pv
            m_ref[:, cols] = m_new

    def query_tile(qi, carry):
        m_ref[...] = jnp.full(m_ref.shape, MASK_NEG, F32)
        acc_ref[...] = jnp.zeros(acc_ref.shape, F32)

        def steady(i, carry):
            j = 3 * i
            scores(buf_b, j + 1, qi)
            consume(buf_a, j, qi, False)
            scores(buf_c, j + 2, qi)
            consume(buf_b, j + 1, qi, False)
            scores(buf_a, j + 3, qi)
            consume(buf_c, j + 2, qi, False)
            return carry

        trips = qi // 3
        lax.fori_loop(0, trips, steady, 0)
        j0 = 3 * trips
        left = qi + 1 - j0

        @pl.when(left == 1)
        def _():
            consume(buf_a, qi, qi, True)

        @pl.when(left == 2)
        def _():
            scores(buf_b, qi, qi)
            consume(buf_a, j0, qi, False)
            consume(buf_b, qi, qi, True)

        @pl.when(left == 3)
        def _():
            scores(buf_b, j0 + 1, qi)
            consume(buf_a, j0, qi, False)
            scores(buf_c, qi, qi)
            consume(buf_b, j0 + 1, qi, False)
            consume(buf_c, qi, qi, True)

        q0 = pl.multiple_of(qi * tq, tq)
        a = acc_ref[0:dv, :] * (1.0 / acc_ref[dv:dv + 1, :])
        scores(buf_a, 0, jnp.minimum(qi + 1, n_q_tiles - 1))
        o = a[:, 0:tq] - lam * a[:, tq:2 * tq]
        ms = jnp.mean(o * o, axis=0, keepdims=True)
        on = o * lax.rsqrt(ms + EPS) * sub_ref[...] * (1.0 - lambda_init)
        gate = gt_ref[0, :, pl.ds(q0, tq)].astype(F32)
        ot_ref[0, :, pl.ds(q0, tq)] = (on * _silu(gate)).astype(BF16)
        return carry

    scores(buf_a, 0, 0)
    lax.fori_loop(0, n_q_tiles, query_tile, 0)


def _out_proj_t_kernel(ot_ref, w_ref, x_ref, g_ref, xo_ref):
    g = g_ref[...]
    for c in range(PROJ_TM // OUT_ROWS):
        rows = slice(c * OUT_ROWS, (c + 1) * OUT_ROWS)
        y = lax.dot_general(ot_ref[0, :, rows], w_ref[...], TN_DIMS, preferred_element_type=F32)
        ms = jnp.mean(y * y, axis=-1, keepdims=True)
        xo_ref[rows, :] = x_ref[rows, :] + y * lax.rsqrt(ms + EPS) * g


def _kv_proj(x2, batch, seq, kv_g, wk, wvt, rope_tabs):
    tokens = x2.shape[0]
    cos, sin = rope_tabs
    nst = seq // PROJ_TM
    return pl.pallas_call(
        _kv_proj_kernel,
        out_shape=(jax.ShapeDtypeStruct((tokens, DIFF_HEADS * ATT_K8_LANES), FP8),
                   jax.ShapeDtypeStruct((batch, nst, DIFF_HEADS, SUBLANES, LANES), F32),
                   jax.ShapeDtypeStruct((batch, DIFF_HEADS * ATT_VT_ROWS, seq), BF16)),
        grid=(batch, nst),
        in_specs=[
            pl.BlockSpec((PROJ_TM, D_MODEL), lambda b, i: (b * nst + i, 0)),
            pl.BlockSpec((1, D_MODEL), lambda b, i: (0, 0)),
            pl.BlockSpec((D_MODEL, DIFF_QK_WIDTH), lambda b, i: (0, 0)),
            pl.BlockSpec((DIFF_WIDTH, D_MODEL), lambda b, i: (0, 0)),
            pl.BlockSpec((PROJ_TM, LANES), lambda b, i: (i, 0)),
            pl.BlockSpec((PROJ_TM, LANES), lambda b, i: (i, 0)),
        ],
        out_specs=(pl.BlockSpec((PROJ_TM, DIFF_HEADS * ATT_K8_LANES), lambda b, i: (b * nst + i, 0)),
                   pl.BlockSpec((1, 1, DIFF_HEADS, SUBLANES, LANES), lambda b, i: (b, i, 0, 0, 0)),
                   pl.BlockSpec((1, DIFF_HEADS * ATT_VT_ROWS, PROJ_TM), lambda b, i: (b, 0, i))),
        compiler_params=_params(("arbitrary", "arbitrary")),
        name="kv_proj",
    )(x2, kv_g, wk, wvt, cos, sin)


def _diff_layer(x2, batch, seq, layer, pre_g, post_g, wt_in, w_out, k8, sk, vt_sh, rope_t_tabs,
                diag_bias, sub_g, lq1, lk1, lq2, lk2):
    tokens = x2.shape[0]
    cos_t, sin_t = rope_t_tabs
    nst = seq // PROJ_TM
    q8, sq, gt = pl.pallas_call(
        _q_proj_kernel,
        out_shape=(jax.ShapeDtypeStruct((batch, DIFF_HEADS * 2 * ATT_Q8_ROWS, seq), FP8),
                   jax.ShapeDtypeStruct((batch, nst, DIFF_HEADS, SUBLANES, LANES), F32),
                   jax.ShapeDtypeStruct((batch, DIFF_WIDTH, seq), BF16)),
        grid=(batch, nst),
        in_specs=[
            pl.BlockSpec((PROJ_TM, D_MODEL), lambda b, i: (b * nst + i, 0)),
            pl.BlockSpec((1, D_MODEL), lambda b, i: (0, 0)),
            pl.BlockSpec((DIFF_QK_WIDTH + DIFF_WIDTH, D_MODEL), lambda b, i: (0, 0)),
            pl.BlockSpec((DIFF_HEAD_DIM // 2, PROJ_TM), lambda b, i: (0, i)),
            pl.BlockSpec((DIFF_HEAD_DIM // 2, PROJ_TM), lambda b, i: (0, i)),
        ],
        out_specs=(pl.BlockSpec((1, DIFF_HEADS * 2 * ATT_Q8_ROWS, PROJ_TM), lambda b, i: (b, 0, i)),
                   pl.BlockSpec((1, 1, DIFF_HEADS, SUBLANES, LANES), lambda b, i: (b, i, 0, 0, 0)),
                   pl.BlockSpec((1, DIFF_WIDTH, PROJ_TM), lambda b, i: (b, 0, i))),
        compiler_params=_params(("arbitrary", "arbitrary")),
        name="q_proj",
    )(x2, pre_g, wt_in, cos_t, sin_t)

    lambda_init = 0.8 - 0.6 * math.exp(-0.3 * layer)
    assert ATT_TQ == PROJ_TM and ATT_TK == PROJ_TM
    nq = seq // ATT_TQ
    k3 = k8.reshape(batch, seq, DIFF_HEADS * ATT_K8_LANES)
    scale_spec = pl.BlockSpec((1, nst, 1, SUBLANES, LANES), lambda b, h: (b, 0, h, 0, 0))
    lam_spec = pl.BlockSpec((1, DIFF_HEAD_DIM), lambda b, h: (0, 0))
    ot = pl.pallas_call(
        functools.partial(_diff_attn_kernel, lambda_init=lambda_init, n_q_tiles=nq),
        out_shape=jax.ShapeDtypeStruct((batch, DIFF_WIDTH, seq), BF16),
        grid=(batch, DIFF_HEADS),
        in_specs=[
            pl.BlockSpec((1, 2 * ATT_Q8_ROWS, seq), lambda b, h: (b, h, 0)),
            scale_spec,
            pl.BlockSpec((1, seq, ATT_K8_LANES), lambda b, h: (b, 0, h)),
            scale_spec,
            pl.BlockSpec((1, ATT_VT_ROWS, seq), lambda b, h: (b, h, 0)),
            pl.BlockSpec((1, DIFF_V_DIM, seq), lambda b, h: (b, h, 0)),
            pl.BlockSpec((DIFF_V_DIM, 1), lambda b, h: (0, 0)),
            pl.BlockSpec((ATT_TK, ATT_TQ), lambda b, h: (0, 0)),
            lam_spec, lam_spec, lam_spec, lam_spec,
        ],
        out_specs=pl.BlockSpec((1, DIFF_V_DIM, seq), lambda b, h: (b, h, 0)),
        scratch_shapes=[
            pltpu.VMEM((DIFF_V_DIM + ATT_ONES_ROWS, 2 * ATT_TQ), F32),
            pltpu.VMEM((1, 2 * ATT_TQ), F32),
            pltpu.VMEM((ATT_TK, 2 * ATT_TQ), BF16),
            pltpu.VMEM((ATT_TK, 2 * ATT_TQ), BF16),
            pltpu.VMEM((ATT_TK, 2 * ATT_TQ), BF16),
            pltpu.VMEM((1, 2 * ATT_TQ), F32),
            pltpu.VMEM((1, 2 * ATT_TQ), F32),
            pltpu.VMEM((1, 2 * ATT_TQ), F32),
        ],
        compiler_params=_params(("arbitrary", "arbitrary")),
        name="diff_attn",
    )(q8, sq, k3, sk, vt_sh, gt, sub_g, diag_bias, lq1, lk1, lq2, lk2)

    return pl.pallas_call(
        _out_proj_t_kernel,
        out_shape=jax.ShapeDtypeStruct((tokens, D_MODEL), F32),
        grid=(batch, nst),
        in_specs=[
            pl.BlockSpec((1, DIFF_WIDTH, PROJ_TM), lambda b, i: (b, 0, i)),
            pl.BlockSpec((DIFF_WIDTH, D_MODEL), lambda b, i: (0, 0)),
            pl.BlockSpec((PROJ_TM, D_MODEL), lambda b, i: (b * nst + i, 0)),
            pl.BlockSpec((1, D_MODEL), lambda b, i: (0, 0)),
        ],
        out_specs=pl.BlockSpec((PROJ_TM, D_MODEL), lambda b, i: (b * nst + i, 0)),
        compiler_params=_params(("arbitrary", "arbitrary")),
        name="diff_out_proj",
    )(ot, w_out, x2, post_g)


def kernel(x, pre_norm, post_norm, w_in_a, w_out_a, kv_norm, w_kv, w_in_b,
           lam_q1, lam_k1, lam_q2, lam_k2, sub_norm_b, w_out_b):
    batch, seq, _ = x.shape
    x2 = x.reshape(batch * seq, D_MODEL)

    ret_rope = _rope_tables_rows(seq, RET_QK_DIM)
    ret_tabs = _retention_tables()
    for layer in range(N_A_LAYERS):
        x2 = _retention_layer(
            x2, batch, seq,
            pre_norm[layer].reshape(1, D_MODEL), post_norm[layer].reshape(1, D_MODEL),
            w_in_a[layer].astype(BF16), w_out_a[layer].astype(BF16), ret_rope, ret_tabs)

    diff_rope = _rope_tables_rows(seq, DIFF_HEAD_DIM)
    k8, sk, vt_sh = _kv_proj(
        x2, batch, seq, kv_norm.reshape(1, D_MODEL),
        w_kv[:, :DIFF_QK_WIDTH].astype(BF16), w_kv[:, DIFF_QK_WIDTH:].T.astype(BF16), diff_rope)

    half = DIFF_HEAD_DIM // 2
    inv = ROPE_THETA ** (-jnp.arange(half, dtype=F32) / half)
    ang_t = inv[:, None] * jnp.arange(seq, dtype=F32)[None, :]
    rope_t = (jnp.cos(ang_t), jnp.sin(ang_t))
    tile_chunk = jnp.arange(ATT_TK, dtype=jnp.int32) // CHUNK
    diag_bias = jnp.where(tile_chunk[:, None] <= tile_chunk[None, :], 0.0, MASK_NEG).astype(BF16)
    for layer in range(N_A_LAYERS, DEPTH):
        j = layer - N_A_LAYERS
        x2 = _diff_layer(
            x2, batch, seq, layer,
            pre_norm[layer].reshape(1, D_MODEL), post_norm[layer].reshape(1, D_MODEL),
            w_in_b[j].T.astype(BF16), w_out_b[j].astype(BF16), k8, sk, vt_sh, rope_t, diag_bias,
            sub_norm_b[j].reshape(DIFF_V_DIM, 1),
            lam_q1[j].reshape(1, DIFF_HEAD_DIM), lam_k1[j].reshape(1, DIFF_HEAD_DIM),
            lam_q2[j].reshape(1, DIFF_HEAD_DIM), lam_k2[j].reshape(1, DIFF_HEAD_DIM))
    return x2.reshape(batch, seq, D_MODEL)
```

```python
import functools
import math

import jax
import jax.numpy as jnp
from jax import lax
from jax.experimental import pallas as pl
from jax.experimental.pallas import tpu as pltpu

F32 = jnp.float32
BF16 = jnp.bfloat16

D_MODEL = 1024
DEPTH = 4
N_A_LAYERS = DEPTH // 2
CHUNK = 64
EPS = 1e-6
ROPE_THETA = 10000.0

RET_HEADS = 8
RET_QK_DIM = 128
RET_V_DIM = 256
RET_QK_WIDTH = RET_HEADS * RET_QK_DIM
RET_WIDTH = RET_HEADS * RET_V_DIM
RET_IN_WIDTH = 2 * RET_QK_WIDTH + 2 * RET_WIDTH

DIFF_HEADS = 8
DIFF_HEAD_DIM = 64
DIFF_V_DIM = 128
DIFF_QK_WIDTH = DIFF_HEADS * 2 * DIFF_HEAD_DIM
DIFF_WIDTH = DIFF_HEADS * DIFF_V_DIM

LANES = 128
SUBLANES = 8
VMEM_LIMIT = 56 * 1024 * 1024

PROJ_TM = 512
PROJ_ROWS = 512
OUT_ROWS = 256
RET_BLOCK = 512
RET_CHUNK = 256
ATT_TQ = 512
ATT_TK = 512
FP8 = jnp.float8_e4m3fn
FP8_TARGET = 256.0
FP8_MIN_AMAX = 2.0 ** -30
ATT_K8_LANES = 8 * DIFF_HEAD_DIM
ATT_Q8_ROWS = 4 * DIFF_HEAD_DIM
ATT_ONES_ROWS = 16
ATT_VT_ROWS = DIFF_V_DIM + ATT_ONES_ROWS
MASK_NEG = -(2.0 ** 100)

NT_DIMS = (((1,), (1,)), ((), ()))
TN_DIMS = (((0,), (0,)), ((), ()))


def _params(sem):
    return pltpu.CompilerParams(dimension_semantics=sem, vmem_limit_bytes=VMEM_LIMIT)


def _normed(x_ref, g_ref):
    x = x_ref[...]
    ms = jnp.mean(x * x, axis=-1, keepdims=True)
    return (x * lax.rsqrt(ms + EPS) * g_ref[...]).astype(BF16)


def _silu(g):
    return g / (1.0 + jnp.exp(-g))


def _retention_layer_kernel(x_ref, pre_g_ref, w_in_ref, cos_ref, sin_ref, dm_ref, qd_ref, kd_ref,
                           w_out_ref, post_g_ref, xo_ref, state_ref, proj_ref, o_ref):
    @pl.when(pl.program_id(1) == 0)
    def _():
        state_ref[...] = jnp.zeros_like(state_ref)

    h = _normed(x_ref, pre_g_ref)
    cos = cos_ref[...]
    sin = sin_ref[...]
    k_scale = RET_QK_DIM ** -0.5
    for j in range(RET_IN_WIDTH // D_MODEL):
        cols = slice(j * D_MODEL, (j + 1) * D_MODEL)
        r = jnp.dot(h, w_in_ref[:, cols], preferred_element_type=F32)
        if j < 2:
            for s in range(RET_HEADS):
                xs = r[:, s * LANES:(s + 1) * LANES]
                y = xs * cos + pltpu.roll(xs, RET_QK_DIM // 2, 1) * sin
                if j == 1:
                    y = y * k_scale
                proj_ref[:, j * D_MODEL + s * LANES:j * D_MODEL + (s + 1) * LANES] = y.astype(BF16)
        else:
            proj_ref[:, cols] = r.astype(BF16)

    k0, v0, g0 = RET_QK_WIDTH, 2 * RET_QK_WIDTH, 2 * RET_QK_WIDTH + RET_WIDTH
    post_g = post_g_ref[...]
    for i in range(RET_BLOCK // RET_CHUNK):
        rows = slice(i * RET_CHUNK, (i + 1) * RET_CHUNK)
        for hd in range(RET_HEADS):
            v_cols = slice(hd * RET_V_DIM, (hd + 1) * RET_V_DIM)
            dm = dm_ref[hd]
            qd = qd_ref[hd]
            kd = kd_ref[hd]
            cdec = qd[RET_CHUNK - 1:RET_CHUNK, :]
            q = proj_ref[rows, hd * RET_QK_DIM:(hd + 1) * RET_QK_DIM]
            k = proj_ref[rows, k0 + hd * RET_QK_DIM:k0 + (hd + 1) * RET_QK_DIM]
            v = proj_ref[rows, v0 + hd * RET_V_DIM:v0 + (hd + 1) * RET_V_DIM]
            gate = proj_ref[rows, g0 + hd * RET_V_DIM:g0 + (hd + 1) * RET_V_DIM]
            s = lax.dot_general(q, k, NT_DIMS, preferred_element_type=F32) * dm
            st = state_ref[hd]
            o = jnp.dot(s.astype(BF16), v, preferred_element_type=F32)
            o = o + qd * jnp.dot(q, st.astype(BF16), preferred_element_type=F32)
            kdk = (k.astype(F32) * kd).astype(BF16)
            state_ref[hd] = st * cdec + lax.dot_general(kdk, v, TN_DIMS, preferred_element_type=F32)
            mu = jnp.mean(o, axis=-1, keepdims=True)
            d = o - mu
            var = jnp.mean(d * d, axis=-1, keepdims=True)
            on = d * lax.rsqrt(var + EPS)
            o_ref[rows, v_cols] = (on * _silu(gate.astype(F32))).astype(BF16)
        y = jnp.dot(o_ref[rows, :], w_out_ref[...], preferred_element_type=F32)
        ms = jnp.mean(y * y, axis=-1, keepdims=True)
        xo_ref[rows, :] = x_ref[rows, :] + y * lax.rsqrt(ms + EPS) * post_g


def _retention_tables():
    h = jnp.arange(RET_HEADS, dtype=F32)
    log_gamma = jnp.log1p(-jnp.exp2(-5.0 - h))
    pos = jnp.arange(RET_CHUNK, dtype=F32)
    diff = pos[:, None] - pos[None, :]
    dm = jnp.where(diff[None] >= 0,
                   jnp.exp(jnp.maximum(diff, 0.0)[None] * log_gamma[:, None, None]), 0.0)
    qd = jnp.exp((pos[None, :] + 1.0) * log_gamma[:, None])
    kd = jnp.exp((RET_CHUNK - 1.0 - pos[None, :]) * log_gamma[:, None])
    qd = jnp.broadcast_to(qd[:, :, None], (RET_HEADS, RET_CHUNK, RET_V_DIM))
    kd = jnp.broadcast_to(kd[:, :, None], (RET_HEADS, RET_CHUNK, RET_QK_DIM))
    return dm, qd, kd


def _rope_tables_rows(seq, dim):
    half = dim // 2
    inv = ROPE_THETA ** (-jnp.arange(half, dtype=F32) / half)
    ang = jnp.arange(seq, dtype=F32)[:, None] * inv[None, :]
    cos = jnp.cos(ang)
    sin = jnp.sin(ang)
    reps = LANES // dim
    cos_full = jnp.tile(jnp.concatenate([cos, cos], axis=1), (1, reps))
    sin_signed = jnp.tile(jnp.concatenate([-sin, sin], axis=1), (1, reps))
    return cos_full, sin_signed


def _retention_layer(x2, batch, seq, pre_g, post_g, w_in, w_out, rope_tabs, ret_tabs):
    tokens = x2.shape[0]
    cos, sin = rope_tabs
    dm, qd, kd = ret_tabs
    nblk = seq // RET_BLOCK

    def resident(shape):
        return pl.BlockSpec(shape, lambda b, c: (0,) * len(shape), pipeline_mode=pl.Buffered(1))

    return pl.pallas_call(
        _retention_layer_kernel,
        out_shape=jax.ShapeDtypeStruct((tokens, D_MODEL), F32),
        grid=(batch, nblk),
        in_specs=[
            pl.BlockSpec((RET_BLOCK, D_MODEL), lambda b, c: (b * nblk + c, 0)),
            resident((1, D_MODEL)),
            resident((D_MODEL, RET_IN_WIDTH)),
            pl.BlockSpec((RET_BLOCK, LANES), lambda b, c: (c, 0)),
            pl.BlockSpec((RET_BLOCK, LANES), lambda b, c: (c, 0)),
            resident((RET_HEADS, RET_CHUNK, RET_CHUNK)),
            resident((RET_HEADS, RET_CHUNK, RET_V_DIM)),
            resident((RET_HEADS, RET_CHUNK, RET_QK_DIM)),
            resident((RET_WIDTH, D_MODEL)),
            resident((1, D_MODEL)),
        ],
        out_specs=pl.BlockSpec((RET_BLOCK, D_MODEL), lambda b, c: (b * nblk + c, 0)),
        scratch_shapes=[pltpu.VMEM((RET_HEADS, RET_QK_DIM, RET_V_DIM), F32),
                        pltpu.VMEM((RET_BLOCK, RET_IN_WIDTH), BF16),
                        pltpu.VMEM((RET_BLOCK, RET_WIDTH), BF16)],
        compiler_params=_params(("arbitrary", "arbitrary")),
        name="retention_layer",
    )(x2, pre_g, w_in, cos, sin, dm, qd, kd, w_out, post_g)


def _absmax(x):
    return jnp.max(jnp.max(jnp.abs(x), axis=0, keepdims=True), axis=1, keepdims=True)


def _pow2_scale(amax):
    e = jnp.floor(jnp.log2(jnp.maximum(amax, FP8_MIN_AMAX))) + (1.0 - math.log2(FP8_TARGET))
    return jnp.exp2(e)


def _hi_lo(x):
    hi = x.astype(FP8).astype(F32)
    lo = (x - hi).astype(FP8).astype(F32)
    return hi, lo


def _kv_proj_kernel(x_ref, g_ref, wk_ref, wvt_ref, cos_ref, sin_ref, k8_ref, sk_ref, vt_ref):
    h = _normed(x_ref, g_ref)
    cos = cos_ref[...]
    sin = sin_ref[...]
    kf = jnp.dot(h, wk_ref[...], preferred_element_type=F32)
    lane = lax.broadcasted_iota(jnp.int32, (PROJ_TM, LANES), 1)
    first_half = (lane & (DIFF_HEAD_DIM - 1)) < (DIFF_HEAD_DIM // 2)
    map1 = lane < DIFF_HEAD_DIM
    for s in range(DIFF_HEADS):
        xs = kf[:, s * LANES:(s + 1) * LANES]
        rot = jnp.where(first_half,
                        pltpu.roll(xs, LANES - DIFF_HEAD_DIM // 2, 1),
                        pltpu.roll(xs, DIFF_HEAD_DIM // 2, 1))
        kr = xs * cos + rot * sin
        sk = _pow2_scale(_absmax(kr))
        hi, lo = _hi_lo(kr * (1.0 / sk))
        lo_sw = pltpu.roll(lo, DIFF_HEAD_DIM, 1)
        c0 = s * ATT_K8_LANES
        k8_ref[:, c0:c0 + LANES] = jnp.where(map1, hi, lo_sw).astype(FP8)
        k8_ref[:, c0 + LANES:c0 + 2 * LANES] = jnp.where(map1, hi, 0.0).astype(FP8)
        k8_ref[:, c0 + 2 * LANES:c0 + 3 * LANES] = jnp.where(map1, lo_sw, hi).astype(FP8)
        k8_ref[:, c0 + 3 * LANES:c0 + 4 * LANES] = jnp.where(map1, 0.0, hi).astype(FP8)
        sk_ref[0, 0, s] = jnp.broadcast_to(sk, (SUBLANES, LANES))
    vt = lax.dot_general(wvt_ref[...], h, NT_DIMS, preferred_element_type=F32)
    ones_rows = jnp.ones((ATT_ONES_ROWS, PROJ_TM), BF16)
    for hd in range(DIFF_HEADS):
        r0 = hd * ATT_VT_ROWS
        vt_ref[0, r0:r0 + DIFF_V_DIM, :] = vt[hd * DIFF_V_DIM:(hd + 1) * DIFF_V_DIM, :].astype(BF16)
        vt_ref[0, r0 + DIFF_V_DIM:r0 + ATT_VT_ROWS, :] = ones_rows


def _q_proj_kernel(x_ref, g_ref, wt_ref, cos_ref, sin_ref, q8_ref, sq_ref, gt_ref):
    h = _normed(x_ref, g_ref)
    cos = cos_ref[...]
    sin = sin_ref[...]
    scale = DIFF_HEAD_DIM ** -0.5 * math.log2(math.e)
    dh = DIFF_HEAD_DIM
    half = dh // 2
    zeros = jnp.zeros((dh, PROJ_TM), FP8)
    heads_per_chunk = PROJ_ROWS // (2 * dh)
    for c in range((DIFF_QK_WIDTH + DIFF_WIDTH) // PROJ_ROWS):
        c0 = c * PROJ_ROWS
        pt = lax.dot_general(wt_ref[c0:c0 + PROJ_ROWS, :], h, NT_DIMS,
                             preferred_element_type=F32)
        if c0 < DIFF_QK_WIDTH:
            for hh in range(heads_per_chunk):
                head = c * heads_per_chunk + hh
                roped = []
                for mp in range(2):
                    r0 = (hh * 2 + mp) * dh
                    x1 = pt[r0:r0 + half, :]
                    x2 = pt[r0 + half:r0 + dh, :]
                    roped.append(((x1 * cos - x2 * sin) * scale, (x1 * sin + x2 * cos) * scale))
                amax = _absmax(roped[0][0])
                for part in (roped[0][1], roped[1][0], roped[1][1]):
                    amax = jnp.maximum(amax, _absmax(part))
                sq = _pow2_scale(amax)
                inv_sq = 1.0 / sq
                sq_ref[0, 0, head] = jnp.broadcast_to(sq, (SUBLANES, LANES))
                for mp in range(2):
                    base = (head * 2 + mp) * ATT_Q8_ROWS
                    split = [_hi_lo(part * inv_sq) for part in roped[mp]]
                    for blk, term in zip((0, 1, 2 + mp), (0, 0, 1)):
                        for hf in range(2):
                            r = base + blk * dh + hf * half
                            q8_ref[0, r:r + half, :] = split[hf][term].astype(FP8)
                    z0 = base + (3 - mp) * dh
                    q8_ref[0, z0:z0 + dh, :] = zeros
        else:
            g0 = c0 - DIFF_QK_WIDTH
            gt_ref[0, g0:g0 + PROJ_ROWS, :] = pt.astype(BF16)


def _diff_attn_kernel(q8_ref, sq_ref, k8_ref, sk_ref, vt_ref, gt_ref, sub_ref, bias_ref,
                      lq1_ref, lk1_ref, lq2_ref, lk2_ref, ot_ref, acc_ref, m_ref,
                      sa_ref, sb_ref, sc_ref, ma_ref, mb_ref, mc_ref, *, lambda_init, n_q_tiles):
    tq, tk = ATT_TQ, ATT_TK
    dv = DIFF_V_DIM
    maps = tuple((slice(mp * ATT_K8_LANES // 2, (mp + 1) * ATT_K8_LANES // 2),
                  slice(mp * ATT_Q8_ROWS, (mp + 1) * ATT_Q8_ROWS),
                  slice(mp * tq, (mp + 1) * tq)) for mp in range(2))
    buf_a, buf_b, buf_c = (sa_ref, ma_ref), (sb_ref, mb_ref), (sc_ref, mc_ref)
    lam = (jnp.exp(jnp.sum(lq1_ref[...] * lk1_ref[...], axis=-1, keepdims=True))
           - jnp.exp(jnp.sum(lq2_ref[...] * lk2_ref[...], axis=-1, keepdims=True))
           + lambda_init)

    def score_scale(j, qi):
        return sk_ref[0, j, 0][0:1, 0:1] * sq_ref[0, qi, 0][0:1, 0:1]

    def scores(buf, j, qi):
        s_ref, tile_max_ref = buf
        kk = pl.multiple_of(j * tk, tk)
        q0 = pl.multiple_of(qi * tq, tq)
        c = score_scale(j, qi)
        for k_lanes, q_rows, cols in maps:
            s = jnp.dot(k8_ref[0, pl.ds(kk, tk), k_lanes], q8_ref[0, q_rows, pl.ds(q0, tq)],
                        preferred_element_type=F32).astype(BF16)
            s_ref[:, cols] = s
            tile_max_ref[:, cols] = jnp.max(s, axis=0, keepdims=True).astype(F32) * c

    def consume(buf, j, qi, diagonal):
        s_ref, tile_max_ref = buf
        kk = pl.multiple_of(j * tk, tk)
        vt = vt_ref[0, :, pl.ds(kk, tk)]
        c = score_scale(j, qi)
        c16 = c.astype(BF16)
        for _, _, cols in maps:
            m_old = m_ref[:, cols]
            if diagonal:
                s = s_ref[:, cols] * c16 + bias_ref[...]
                m_new = jnp.maximum(m_old, jnp.max(s, axis=0, keepdims=True).astype(F32))
                p = jnp.exp2(s - m_new.astype(BF16))
            else:
                m_new = jnp.maximum(m_old, tile_max_ref[:, cols])
                p = jnp.exp2(s_ref[:, cols] * c16 - m_new.astype(BF16))
            alpha = jnp.exp2(m_old - m_new)
            pv = jnp.dot(vt, p, preferred_element_type=F32)
            acc_ref[:, cols] = alpha * acc_ref[:, cols] + pv
            m_ref[:, cols] = m_new

    def query_tile(qi, carry):
        m_ref[...] = jnp.full(m_ref.shape, MASK_NEG, F32)
        acc_ref[...] = jnp.zeros(acc_ref.shape, F32)

        def steady(i, carry):
            j = 3 * i
            scores(buf_b, j + 1, qi)
            consume(buf_a, j, qi, False)
            scores(buf_c, j + 2, qi)
            consume(buf_b, j + 1, qi, False)
            scores(buf_a, j + 3, qi)
            consume(buf_c, j + 2, qi, False)
            return carry

        trips = qi // 3
        lax.fori_loop(0, trips, steady, 0)
        j0 = 3 * trips
        left = qi + 1 - j0

        @pl.when(left == 1)
        def _():
            consume(buf_a, qi, qi, True)

        @pl.when(left == 2)
        def _():
            scores(buf_b, qi, qi)
            consume(buf_a, j0, qi, False)
            consume(buf_b, qi, qi, True)

        @pl.when(left == 3)
        def _():
            scores(buf_b, j0 + 1, qi)
            consume(buf_a, j0, qi, False)
            scores(buf_c, qi, qi)
            consume(buf_b, j0 + 1, qi, False)
            consume(buf_c, qi, qi, True)

        q0 = pl.multiple_of(qi * tq, tq)
        a = acc_ref[0:dv, :] * (1.0 / acc_ref[dv:dv + 1, :])
        scores(buf_a, 0, jnp.minimum(qi + 1, n_q_tiles - 1))
        o = a[:, 0:tq] - lam * a[:, tq:2 * tq]
        ms = jnp.mean(o * o, axis=0, keepdims=True)
        on = o * lax.rsqrt(ms + EPS) * sub_ref[...] * (1.0 - lambda_init)
        gate = gt_ref[0, :, pl.ds(q0, tq)].astype(F32)
        ot_ref[0, :, pl.ds(q0, tq)] = (on * _silu(gate)).astype(BF16)
        return carry

    scores(buf_a, 0, 0)
    lax.fori_loop(0, n_q_tiles, query_tile, 0)


def _out_proj_t_kernel(ot_ref, w_ref, x_ref, g_ref, xo_ref):
    g = g_ref[...]
    for c in range(PROJ_TM // OUT_ROWS):
        rows = slice(c * OUT_ROWS, (c + 1) * OUT_ROWS)
        y = lax.dot_general(ot_ref[0, :, rows], w_ref[...], TN_DIMS, preferred_element_type=F32)
        ms = jnp.mean(y * y, axis=-1, keepdims=True)
        xo_ref[rows, :] = x_ref[rows, :] + y * lax.rsqrt(ms + EPS) * g


def _kv_proj(x2, batch, seq, kv_g, wk, wvt, rope_tabs):
    tokens = x2.shape[0]
    cos, sin = rope_tabs
    nst = seq // PROJ_TM
    return pl.pallas_call(
        _kv_proj_kernel,
        out_shape=(jax.ShapeDtypeStruct((tokens, DIFF_HEADS * ATT_K8_LANES), FP8),
                   jax.ShapeDtypeStruct((batch, nst, DIFF_HEADS, SUBLANES, LANES), F32),
                   jax.ShapeDtypeStruct((batch, DIFF_HEADS * ATT_VT_ROWS, seq), BF16)),
        grid=(batch, nst),
        in_specs=[
            pl.BlockSpec((PROJ_TM, D_MODEL), lambda b, i: (b * nst + i, 0)),
            pl.BlockSpec((1, D_MODEL), lambda b, i: (0, 0)),
            pl.BlockSpec((D_MODEL, DIFF_QK_WIDTH), lambda b, i: (0, 0)),
            pl.BlockSpec((DIFF_WIDTH, D_MODEL), lambda b, i: (0, 0)),
            pl.BlockSpec((PROJ_TM, LANES), lambda b, i: (i, 0)),
            pl.BlockSpec((PROJ_TM, LANES), lambda b, i: (i, 0)),
        ],
        out_specs=(pl.BlockSpec((PROJ_TM, DIFF_HEADS * ATT_K8_LANES), lambda b, i: (b * nst + i, 0)),
                   pl.BlockSpec((1, 1, DIFF_HEADS, SUBLANES, LANES), lambda b, i: (b, i, 0, 0, 0)),
                   pl.BlockSpec((1, DIFF_HEADS * ATT_VT_ROWS, PROJ_TM), lambda b, i: (b, 0, i))),
        compiler_params=_params(("arbitrary", "arbitrary")),
        name="kv_proj",
    )(x2, kv_g, wk, wvt, cos, sin)


def _diff_layer(x2, batch, seq, layer, pre_g, post_g, wt_in, w_out, k8, sk, vt_sh, rope_t_tabs,
                diag_bias, sub_g, lq1, lk1, lq2, lk2):
    tokens = x2.shape[0]
    cos_t, sin_t = rope_t_tabs
    nst = seq // PROJ_TM
    q8, sq, gt = pl.pallas_call(
        _q_proj_kernel,
        out_shape=(jax.ShapeDtypeStruct((batch, DIFF_HEADS * 2 * ATT_Q8_ROWS, seq), FP8),
                   jax.ShapeDtypeStruct((batch, nst, DIFF_HEADS, SUBLANES, LANES), F32),
                   jax.ShapeDtypeStruct((batch, DIFF_WIDTH, seq), BF16)),
        grid=(batch, nst),
        in_specs=[
            pl.BlockSpec((PROJ_TM, D_MODEL), lambda b, i: (b * nst + i, 0)),
            pl.BlockSpec((1, D_MODEL), lambda b, i: (0, 0)),
            pl.BlockSpec((DIFF_QK_WIDTH + DIFF_WIDTH, D_MODEL), lambda b, i: (0, 0)),
            pl.BlockSpec((DIFF_HEAD_DIM // 2, PROJ_TM), lambda b, i: (0, i)),
            pl.BlockSpec((DIFF_HEAD_DIM // 2, PROJ_TM), lambda b, i: (0, i)),
        ],
        out_specs=(pl.BlockSpec((1, DIFF_HEADS * 2 * ATT_Q8_ROWS, PROJ_TM), lambda b, i: (b, 0, i)),
                   pl.BlockSpec((1, 1, DIFF_HEADS, SUBLANES, LANES), lambda b, i: (b, i, 0, 0, 0)),
                   pl.BlockSpec((1, DIFF_WIDTH, PROJ_TM), lambda b, i: (b, 0, i))),
        compiler_params=_params(("arbitrary", "arbitrary")),
        name="q_proj",
    )(x2, pre_g, wt_in, cos_t, sin_t)

    lambda_init = 0.8 - 0.6 * math.exp(-0.3 * layer)
    assert ATT_TQ == PROJ_TM and ATT_TK == PROJ_TM
    nq = seq // ATT_TQ
    k3 = k8.reshape(batch, seq, DIFF_HEADS * ATT_K8_LANES)
    scale_spec = pl.BlockSpec((1, nst, 1, SUBLANES, LANES), lambda b, h: (b, 0, h, 0, 0))
    lam_spec = pl.BlockSpec((1, DIFF_HEAD_DIM), lambda b, h: (0, 0))
    ot = pl.pallas_call(
        functools.partial(_diff_attn_kernel, lambda_init=lambda_init, n_q_tiles=nq),
        out_shape=jax.ShapeDtypeStruct((batch, DIFF_WIDTH, seq), BF16),
        grid=(batch, DIFF_HEADS),
        in_specs=[
            pl.BlockSpec((1, 2 * ATT_Q8_ROWS, seq), lambda b, h: (b, h, 0)),
            scale_spec,
            pl.BlockSpec((1, seq, ATT_K8_LANES), lambda b, h: (b, 0, h)),
            scale_spec,
            pl.BlockSpec((1, ATT_VT_ROWS, seq), lambda b, h: (b, h, 0)),
            pl.BlockSpec((1, DIFF_V_DIM, seq), lambda b, h: (b, h, 0)),
            pl.BlockSpec((DIFF_V_DIM, 1), lambda b, h: (0, 0)),
            pl.BlockSpec((ATT_TK, ATT_TQ), lambda b, h: (0, 0)),
            lam_spec, lam_spec, lam_spec, lam_spec,
        ],
        out_specs=pl.BlockSpec((1, DIFF_V_DIM, seq), lambda b, h: (b, h, 0)),
        scratch_shapes=[
            pltpu.VMEM((DIFF_V_DIM + ATT_ONES_ROWS, 2 * ATT_TQ), F32),
            pltpu.VMEM((1, 2 * ATT_TQ), F32),
            pltpu.VMEM((ATT_TK, 2 * ATT_TQ), BF16),
            pltpu.VMEM((ATT_TK, 2 * ATT_TQ), BF16),
            pltpu.VMEM((ATT_TK, 2 * ATT_TQ), BF16),
            pltpu.VMEM((1, 2 * ATT_TQ), F32),
            pltpu.VMEM((1, 2 * ATT_TQ), F32),
            pltpu.VMEM((1, 2 * ATT_TQ), F32),
        ],
        compiler_params=_params(("arbitrary", "arbitrary")),
        name="diff_attn",
    )(q8, sq, k3, sk, vt_sh, gt, sub_g, diag_bias, lq1, lk1, lq2, lk2)

    return pl.pallas_call(
        _out_proj_t_kernel,
        out_shape=jax.ShapeDtypeStruct((tokens, D_MODEL), F32),
        grid=(batch, nst),
        in_specs=[
            pl.BlockSpec((1, DIFF_WIDTH, PROJ_TM), lambda b, i: (b, 0, i)),
            pl.BlockSpec((DIFF_WIDTH, D_MODEL), lambda b, i: (0, 0)),
            pl.BlockSpec((PROJ_TM, D_MODEL), lambda b, i: (b * nst + i, 0)),
            pl.BlockSpec((1, D_MODEL), lambda b, i: (0, 0)),
        ],
        out_specs=pl.BlockSpec((PROJ_TM, D_MODEL), lambda b, i: (b * nst + i, 0)),
        compiler_params=_params(("arbitrary", "arbitrary")),
        name="diff_out_proj",
    )(ot, w_out, x2, post_g)


def kernel(x, pre_norm, post_norm, w_in_a, w_out_a, kv_norm, w_kv, w_in_b,
           lam_q1, lam_k1, lam_q2, lam_k2, sub_norm_b, w_out_b):
    batch, seq, _ = x.shape
    x2 = x.reshape(batch * seq, D_MODEL)

    ret_rope = _rope_tables_rows(seq, RET_QK_DIM)
    ret_tabs = _retention_tables()
    for layer in range(N_A_LAYERS):
        x2 = _retention_layer(
            x2, batch, seq,
            pre_norm[layer].reshape(1, D_MODEL), post_norm[layer].reshape(1, D_MODEL),
            w_in_a[layer].astype(BF16), w_out_a[layer].astype(BF16), ret_rope, ret_tabs)

    diff_rope = _rope_tables_rows(seq, DIFF_HEAD_DIM)
    k8, sk, vt_sh = _kv_proj(
        x2, batch, seq, kv_norm.reshape(1, D_MODEL),
        w_kv[:, :DIFF_QK_WIDTH].astype(BF16), w_kv[:, DIFF_QK_WIDTH:].T.astype(BF16), diff_rope)

    half = DIFF_HEAD_DIM // 2
    inv = ROPE_THETA ** (-jnp.arange(half, dtype=F32) / half)
    ang_t = inv[:, None] * jnp.arange(seq, dtype=F32)[None, :]
    rope_t = (jnp.cos(ang_t), jnp.sin(ang_t))
    tile_chunk = jnp.arange(ATT_TK, dtype=jnp.int32) // CHUNK
    diag_bias = jnp.where(tile_chunk[:, None] <= tile_chunk[None, :], 0.0, MASK_NEG).astype(BF16)
    for layer in range(N_A_LAYERS, DEPTH):
        j = layer - N_A_LAYERS
        x2 = _diff_layer(
            x2, batch, seq, layer,
            pre_norm[layer].reshape(1, D_MODEL), post_norm[layer].reshape(1, D_MODEL),
            w_in_b[j].T.astype(BF16), w_out_b[j].astype(BF16), k8, sk, vt_sh, rope_t, diag_bias,
            sub_norm_b[j].reshape(DIFF_V_DIM, 1),
            lam_q1[j].reshape(1, DIFF_HEAD_DIM), lam_k1[j].reshape(1, DIFF_HEAD_DIM),
            lam_q2[j].reshape(1, DIFF_HEAD_DIM), lam_k2[j].reshape(1, DIFF_HEAD_DIM))
    return x2.reshape(batch, seq, D_MODEL)
```
